```python
import math
import jax
import jax.numpy as jnp
from jax import lax
import numpy as np

D_MODEL = 1024
BATCH = 2
SEQ = 8192
DEPTH = 2
DEC_BATCH = 8
DEC_SEQ = 64
PAST_LEN = 4096

CHUNK = 64
N_EVEN = (DEPTH + 1) // 2
N_ODD = DEPTH // 2
EPS = 1e-6

A_WIDTH = D_MODEL
A_CONV = 3
SSM_HEADS = 32
SSM_HEAD_DIM = 64
SSM_INNER = SSM_HEADS * SSM_HEAD_DIM
SSM_STATE = 128
SSM_GROUPS = 4
SSM_CONV = 4
SSM_XBC = SSM_INNER + 2 * SSM_GROUPS * SSM_STATE
IN_EVEN = 3 * A_WIDTH + SSM_INNER + SSM_XBC + SSM_HEADS
MIX_EVEN = A_WIDTH + SSM_INNER
MLA_HEADS = 16
MLA_Q_LORA = 512
MLA_KV_LORA = 256
MLA_NOPE = 64
MLA_ROPE = 32
MLA_V = 64
MLA_SCALE = (MLA_NOPE + MLA_ROPE) ** -0.5
ROPE_THETA = 10000.0
Q_BLOCK = 128
MEM_TOKENS = 256
MEM_HEADS = 4
MEM_HEAD_DIM = D_MODEL // MEM_HEADS
FFN_DENSE = 2816
N_EXPERTS = 8
TOP_K = 2
FFN_EXPERT = 3584

kernel_name = 'hybrid_streaming_encoder_step'


def rmsnorm(x, g):
    xf = x.astype(jnp.float32)
    y = xf * lax.rsqrt(jnp.mean(xf * xf, axis=-1, keepdims=True) + EPS)
    return (y * g.astype(jnp.float32)).astype(x.dtype)


def rope(x, pos):
    half = x.shape[-1] // 2
    inv = ROPE_THETA ** (-jnp.arange(half, dtype=jnp.float32) / half)
    ang = pos.astype(jnp.float32)[:, None] * inv[None, :]
    cos = jnp.cos(ang)[:, None, :]
    sin = jnp.sin(ang)[:, None, :]
    xf = x.astype(jnp.float32)
    x1, x2 = xf[..., :half], xf[..., half:]
    return jnp.concatenate([x1 * cos - x2 * sin, x1 * sin + x2 * cos], axis=-1).astype(x.dtype)


def causal_dwconv(u, w, prev):
    K = w.shape[0]
    L = u.shape[1]
    up = jnp.concatenate([prev.astype(u.dtype), u], axis=1)
    y = w[0] * up[:, 0:L]
    for k in range(1, K):
        y = y + w[k] * up[:, k:k + L]
    return y, up[:, up.shape[1] - (K - 1):]


def ssd_scan(xh, dt, a_neg, bg, cg, h0):
    f32 = jnp.float32
    b, l, nh, p = xh.shape
    g, n = bg.shape[2], bg.shape[3]
    r = nh // g
    q = min(CHUNK, l)
    c = l // q
    la = (dt * a_neg).reshape(b, c, q, g, r)
    xdt = (xh.astype(f32) * dt[..., None]).reshape(b, c, q, g, r, p)
    bc = bg.astype(f32).reshape(b, c, q, g, n)
    cc = cg.astype(f32).reshape(b, c, q, g, n)
    cs = jnp.cumsum(la, axis=2)
    tril = jnp.tril(jnp.ones((q, q), dtype=bool))[None, None, :, :, None, None]
    seg = cs[:, :, :, None] - cs[:, :, None, :]
    decay_in = jnp.exp(jnp.where(tril, seg, -jnp.inf))
    scores = jnp.einsum('bcign,bcjgn->bcijg', cc, bc)
    y_diag = jnp.einsum('bcijgr,bcjgrp->bcigrp', scores[..., None] * decay_in, xdt)
    w_state = xdt * jnp.exp(cs[:, :, -1:] - cs)[..., None]
    chunk_states = jnp.einsum('bcjgn,bcjgrp->bcgrpn', bc, w_state)
    chunk_decay = jnp.exp(cs[:, :, -1])

    def carry(hc, inp):
        st, dec = inp
        return hc * dec[..., None, None] + st, hc

    h_fin, h_in = lax.scan(carry, h0.astype(f32).reshape(b, g, r, p, n),
                           (jnp.moveaxis(chunk_states, 1, 0), jnp.moveaxis(chunk_decay, 1, 0)))
    h_in = jnp.moveaxis(h_in, 0, 1)
    y_off = jnp.einsum('bcign,bcgrpn->bcigrp', cc, h_in) * jnp.exp(cs)[..., None]
    return (y_diag + y_off).reshape(b, l, nh, p), h_fin.reshape(b, nh, p, n)


def mixer_even(h, w_in, conv_a_w, conv_s_w, conv_s_b, dt_bias, a_log, d_skip, ssm_norm, w_out,
               conv_a_prev, conv_s_prev, ssm_prev):
    b, l, _ = h.shape
    proj = h @ w_in
    cuts = [A_WIDTH, 2 * A_WIDTH, 3 * A_WIDTH, 3 * A_WIDTH + SSM_INNER,
            3 * A_WIDTH + SSM_INNER + SSM_XBC]
    g_b, g_c, u, z, xbc, dt_raw = jnp.split(proj, cuts, axis=-1)
    conv_a, conv_a_new = causal_dwconv(g_c * u, conv_a_w, conv_a_prev)
    y_a = g_b * conv_a
    xbc_c, conv_s_new = causal_dwconv(xbc, conv_s_w, conv_s_prev)
    xbc_c = jax.nn.silu(xbc_c + conv_s_b)
    xs, bs, cs = jnp.split(xbc_c, [SSM_INNER, SSM_INNER + SSM_GROUPS * SSM_STATE], axis=-1)
    dt = jax.nn.softplus(dt_raw.astype(jnp.float32) + dt_bias.astype(jnp.float32))
    a_neg = -jnp.exp(a_log.astype(jnp.float32))
    xh = xs.reshape(b, l, SSM_HEADS, SSM_HEAD_DIM)
    y, ssm_new = ssd_scan(xh, dt, a_neg, bs.reshape(b, l, SSM_GROUPS, SSM_STATE),
                          cs.reshape(b, l, SSM_GROUPS, SSM_STATE), ssm_prev)
    y = y + d_skip.astype(jnp.float32)[:, None] * xh.astype(jnp.float32)
    y = y.reshape(b, l, SSM_INNER).astype(h.dtype) * jax.nn.silu(z)
    y = rmsnorm(y.reshape(b, l, SSM_GROUPS, SSM_INNER // SSM_GROUPS),
                ssm_norm.reshape(SSM_GROUPS, SSM_INNER // SSM_GROUPS)).reshape(b, l, SSM_INNER)
    out = jnp.concatenate([y_a, y], axis=-1) @ w_out
    return out, conv_a_new, conv_s_new, ssm_new


def mla_rows(h, pos, w_down, q_lora_norm, kv_lora_norm, w_uq, q_nope_norm, q_rope_norm, k_rope_norm):
    b, l, _ = h.shape
    cq, ckv, kr = jnp.split(h @ w_down, [MLA_Q_LORA, MLA_Q_LORA + MLA_KV_LORA], axis=-1)
    cq = rmsnorm(cq, q_lora_norm)
    ckv = rmsnorm(ckv, kv_lora_norm)
    q = (cq @ w_uq).reshape(b, l, MLA_HEADS, MLA_NOPE + MLA_ROPE)
    q_nope = rmsnorm(q[..., :MLA_NOPE], q_nope_norm)
    q_rope = rope(rmsnorm(q[..., MLA_NOPE:], q_rope_norm), pos)
    k_rope = rope(rmsnorm(kr, k_rope_norm)[:, :, None, :], pos)[:, :, 0, :]
    return q_nope, q_rope, ckv, k_rope


def mla_keys(ckv, w_ukv, k_nope_norm):
    b, n, _ = ckv.shape
    kv = (ckv @ w_ukv).reshape(b, n, MLA_HEADS, MLA_NOPE + MLA_V)
    return rmsnorm(kv[..., :MLA_NOPE], k_nope_norm), kv[..., MLA_NOPE:]


def mla_attend(qn, qr, k_nope, k_rope, v, mask):
    s = (jnp.einsum('bqhd,bkhd->bhqk', qn, k_nope)
         + jnp.einsum('bqhd,bkd->bhqk', qr, k_rope)).astype(jnp.float32) * MLA_SCALE
    if mask is not None:
        s = jnp.where(mask, s, -jnp.inf)
    p = jax.nn.softmax(s, axis=-1).astype(v.dtype)
    return jnp.einsum('bhqk,bkhd->bqhd', p, v)


def mla_prompt(h, pos, w_down, q_lora_norm, kv_lora_norm, w_uq, w_ukv, q_nope_norm, q_rope_norm,
               k_nope_norm, k_rope_norm, w_o):
    b, l, _ = h.shape
    q_nope, q_rope, ckv, k_rope = mla_rows(h, pos, w_down, q_lora_norm, kv_lora_norm, w_uq,
                                           q_nope_norm, q_rope_norm, k_rope_norm)
    k_nope, v = mla_keys(ckv, w_ukv, k_nope_norm)
    key_chunk = jnp.arange(l) // CHUNK

    def block(i):
        start = i * Q_BLOCK
        qn = lax.dynamic_slice_in_dim(q_nope, start, Q_BLOCK, axis=1)
        qr = lax.dynamic_slice_in_dim(q_rope, start, Q_BLOCK, axis=1)
        q_chunk = (start + jnp.arange(Q_BLOCK)) // CHUNK
        mask = (key_chunk[None, :] <= q_chunk[:, None])[None, None]
        return mla_attend(qn, qr, k_nope, k_rope, v, mask)

    o = lax.map(block, jnp.arange(l // Q_BLOCK))
    o = jnp.moveaxis(o, 0, 1).reshape(b, l, MLA_HEADS * MLA_V)
    return o @ w_o, ckv, k_rope


def mla_sample(h, pos, ckv_past, kr_past, w_down, q_lora_norm, kv_lora_norm, w_uq, w_ukv, q_nope_norm,
               q_rope_norm, k_nope_norm, k_rope_norm, w_o):
    b, l, _ = h.shape
    q_nope, q_rope, ckv, k_rope = mla_rows(h, pos, w_down, q_lora_norm, kv_lora_norm, w_uq,
                                           q_nope_norm, q_rope_norm, k_rope_norm)
    ckv_all = jnp.concatenate([ckv_past.astype(ckv.dtype), ckv], axis=1)
    kr_all = jnp.concatenate([kr_past.astype(k_rope.dtype), k_rope], axis=1)
    k_nope, v = mla_keys(ckv_all, w_ukv, k_nope_norm)
    o = mla_attend(q_nope, q_rope, k_nope, kr_all, v, None).reshape(b, l, MLA_HEADS * MLA_V)
    return o @ w_o, ckv, k_rope


def mem_kv(mem, src_norm, w_k, w_v, k_norm):
    b, m, _ = mem.shape
    mm = rmsnorm(mem, src_norm)
    k = rmsnorm((mm @ w_k).reshape(b, m, MEM_HEADS, MEM_HEAD_DIM), k_norm)
    v = (mm @ w_v).reshape(b, m, MEM_HEADS, MEM_HEAD_DIM)
    return k, v


def mem_attend(h, k, v, w_q, q_norm, w_o):
    b, l, _ = h.shape
    q = rmsnorm((h @ w_q).reshape(b, l, MEM_HEADS, MEM_HEAD_DIM), q_norm)
    s = jnp.einsum('blhd,bmhd->bhlm', q, k.astype(q.dtype)).astype(jnp.float32) * MEM_HEAD_DIM ** -0.5
    p = jax.nn.softmax(s, axis=-1).astype(q.dtype)
    o = jnp.einsum('bhlm,bmhd->blhd', p, v.astype(q.dtype)).reshape(b, l, MEM_HEADS * MEM_HEAD_DIM)
    return o @ w_o


def swiglu(h, w1, w3, w2):
    return (jax.nn.silu(h @ w1) * (h @ w3)) @ w2


def moe_ffn(h, w_router, w1, w3, w2):
    logits = (h @ w_router).astype(jnp.float32)
    top_v, top_i = lax.top_k(logits, TOP_K)
    gates_k = jax.nn.softmax(top_v, axis=-1)
    gates = jnp.einsum('blk,blke->ble', gates_k, jax.nn.one_hot(top_i, N_EXPERTS, dtype=jnp.float32))
    y = gates[..., 0:1].astype(h.dtype) * swiglu(h, w1[0], w3[0], w2[0])
    for e in range(1, N_EXPERTS):
        y = y + gates[..., e:e + 1].astype(h.dtype) * swiglu(h, w1[e], w3[e], w2[e])
    return y


def setup_inputs(seed: int = 0) -> dict:
    key = jax.random.key(seed)
    ks = iter(jax.random.split(key, 96))
    D = D_MODEL

    def nrm(shape, scale=1.0):
        return jax.random.normal(next(ks), shape, jnp.float32) * scale

    def gain(shape):
        return 1.0 + nrm(shape, 0.02)

    dt0 = jnp.exp(jax.random.uniform(next(ks), (N_EVEN, SSM_HEADS), jnp.float32,
                                     math.log(1e-3), math.log(1e-1)))
    a_init = jax.random.uniform(next(ks), (N_EVEN, SSM_HEADS), jnp.float32, 1.0, 16.0)
    inp = {
        'x_prompt': nrm((BATCH, SEQ, D)),
        'x_sample': nrm((DEC_BATCH, DEC_SEQ, D)),
        'mem_prompt': nrm((BATCH, MEM_TOKENS, D)),
        'state_conv_a': nrm((N_EVEN, DEC_BATCH, A_CONV - 1, A_WIDTH)),
        'state_conv_ssm': nrm((N_EVEN, DEC_BATCH, SSM_CONV - 1, SSM_XBC)),
        'state_ssm': nrm((N_EVEN, DEC_BATCH, SSM_HEADS, SSM_HEAD_DIM, SSM_STATE), 0.5),
        'cache_mla_ckv': nrm((N_ODD, DEC_BATCH, PAST_LEN, MLA_KV_LORA)),
        'cache_mla_krope': nrm((N_ODD, DEC_BATCH, PAST_LEN, MLA_ROPE)),
        'cache_mem_k': nrm((DEPTH, DEC_BATCH, MEM_TOKENS, MEM_HEADS, MEM_HEAD_DIM)),
        'cache_mem_v': nrm((DEPTH, DEC_BATCH, MEM_TOKENS, MEM_HEADS, MEM_HEAD_DIM)),
        'norm_mix': gain((DEPTH, D)),
        'norm_mem': gain((DEPTH, D)),
        'norm_ffn': gain((DEPTH, D)),
        'w_in_e': nrm((N_EVEN, D, IN_EVEN), D ** -0.5),
        'conv_a_w': nrm((N_EVEN, A_CONV, A_WIDTH), A_CONV ** -0.5),
        'conv_s_w': nrm((N_EVEN, SSM_CONV, SSM_XBC), SSM_CONV ** -0.5),
        'conv_s_b': nrm((N_EVEN, SSM_XBC), 0.02),
        'dt_bias': dt0 + jnp.log(-jnp.expm1(-dt0)),
        'a_log': jnp.log(a_init),
        'd_skip': 1.0 + nrm((N_EVEN, SSM_HEADS), 0.1),
        'ssm_norm': gain((N_EVEN, SSM_INNER)),
        'w_out_e': nrm((N_EVEN, MIX_EVEN, D), MIX_EVEN ** -0.5),
        'w_down_o': nrm((N_ODD, D, MLA_Q_LORA + MLA_KV_LORA + MLA_ROPE), D ** -0.5),
        'q_lora_norm': gain((N_ODD, MLA_Q_LORA)),
        'kv_lora_norm': gain((N_ODD, MLA_KV_LORA)),
        'w_uq': nrm((N_ODD, MLA_Q_LORA, MLA_HEADS * (MLA_NOPE + MLA_ROPE)), MLA_Q_LORA ** -0.5),
        'w_ukv': nrm((N_ODD, MLA_KV_LORA, MLA_HEADS * (MLA_NOPE + MLA_V)), MLA_KV_LORA ** -0.5),
        'q_nope_norm': gain((N_ODD, MLA_NOPE)),
        'q_rope_norm': gain((N_ODD, MLA_ROPE)),
        'k_nope_norm': gain((N_ODD, MLA_NOPE)),
        'k_rope_norm': gain((N_ODD, MLA_ROPE)),
        'w_o_mla': nrm((N_ODD, MLA_HEADS * MLA_V, D), (MLA_HEADS * MLA_V) ** -0.5),
        'mem_src_norm': gain((DEPTH, D)),
        'w_mem_q': nrm((DEPTH, D, D), D ** -0.5),
        'w_mem_k': nrm((DEPTH, D, D), D ** -0.5),
        'w_mem_v': nrm((DEPTH, D, D), D ** -0.5),
        'mem_q_norm': gain((DEPTH, MEM_HEAD_DIM)),
        'mem_k_norm': gain((DEPTH, MEM_HEAD_DIM)),
        'w_mem_o': nrm((DEPTH, D, D), D ** -0.5),
        'w_ffn1': nrm((N_EVEN, D, FFN_DENSE), D ** -0.5),
        'w_ffn3': nrm((N_EVEN, D, FFN_DENSE), D ** -0.5),
        'w_ffn2': nrm((N_EVEN, FFN_DENSE, D), FFN_DENSE ** -0.5),
        'w_router': nrm((N_ODD, D, N_EXPERTS), D ** -0.5),
        'w_exp1': nrm((N_ODD, N_EXPERTS, D, FFN_EXPERT), D ** -0.5),
        'w_exp3': nrm((N_ODD, N_EXPERTS, D, FFN_EXPERT), D ** -0.5),
        'w_exp2': nrm((N_ODD, N_EXPERTS, FFN_EXPERT, D), FFN_EXPERT ** -0.5),
    }
    return inp


def reference(x_prompt, x_sample, mem_prompt, state_conv_a, state_conv_ssm, state_ssm,
              cache_mla_ckv, cache_mla_krope, cache_mem_k, cache_mem_v,
              norm_mix, norm_mem, norm_ffn,
              w_in_e, conv_a_w, conv_s_w, conv_s_b, dt_bias, a_log, d_skip, ssm_norm, w_out_e,
              w_down_o, q_lora_norm, kv_lora_norm, w_uq, w_ukv, q_nope_norm, q_rope_norm,
              k_nope_norm, k_rope_norm, w_o_mla,
              mem_src_norm, w_mem_q, w_mem_k, w_mem_v, mem_q_norm, mem_k_norm, w_mem_o,
              w_ffn1, w_ffn3, w_ffn2,
              w_router, w_exp1, w_exp3, w_exp2):
    bp, lp, _ = x_prompt.shape
    bs, ls, _ = x_sample.shape
    past = cache_mla_ckv.shape[2]
    pos_p = jnp.arange(lp)
    pos_s = past + jnp.arange(ls)
    xp, xs = x_prompt, x_sample
    p_conv_a, p_conv_s, p_ssm, p_ckv, p_kr, p_mk, p_mv = [], [], [], [], [], [], []
    s_conv_a, s_conv_s, s_ssm, s_ckv, s_kr = [], [], [], [], []
    for i in range(DEPTH):
        j = i // 2
        hp = rmsnorm(xp, norm_mix[i])
        hs = rmsnorm(xs, norm_mix[i])
        if i % 2 == 0:
            wts = (w_in_e[j], conv_a_w[j], conv_s_w[j], conv_s_b[j], dt_bias[j], a_log[j],
                   d_skip[j], ssm_norm[j], w_out_e[j])
            op, ca, cs, st = mixer_even(
                hp, *wts,
                jnp.zeros((bp, A_CONV - 1, A_WIDTH), xp.dtype),
                jnp.zeros((bp, SSM_CONV - 1, SSM_XBC), xp.dtype),
                jnp.zeros((bp, SSM_HEADS, SSM_HEAD_DIM, SSM_STATE), jnp.float32))
            p_conv_a.append(ca)
            p_conv_s.append(cs)
            p_ssm.append(st)
            o_s, ca, cs, st = mixer_even(hs, *wts, state_conv_a[j], state_conv_ssm[j], state_ssm[j])
            s_conv_a.append(ca)
            s_conv_s.append(cs)
            s_ssm.append(st)
        else:
            wts = (w_down_o[j], q_lora_norm[j], kv_lora_norm[j], w_uq[j], w_ukv[j], q_nope_norm[j],
                   q_rope_norm[j], k_nope_norm[j], k_rope_norm[j], w_o_mla[j])
            op, ckv, kr = mla_prompt(hp, pos_p, *wts)
            p_ckv.append(ckv)
            p_kr.append(kr)
            o_s, ckv, kr = mla_sample(hs, pos_s, cache_mla_ckv[j], cache_mla_krope[j], *wts)
            s_ckv.append(ckv)
            s_kr.append(kr)
        xp = xp + op
        xs = xs + o_s
        mk, mv = mem_kv(mem_prompt, mem_src_norm[i], w_mem_k[i], w_mem_v[i], mem_k_norm[i])
        p_mk.append(mk)
        p_mv.append(mv)
        xp = xp + mem_attend(rmsnorm(xp, norm_mem[i]), mk, mv, w_mem_q[i], mem_q_norm[i], w_mem_o[i])
        xs = xs + mem_attend(rmsnorm(xs, norm_mem[i]), cache_mem_k[i], cache_mem_v[i],
                             w_mem_q[i], mem_q_norm[i], w_mem_o[i])
        if i % 2 == 0:
            xp = xp + swiglu(rmsnorm(xp, norm_ffn[i]), w_ffn1[j], w_ffn3[j], w_ffn2[j])
            xs = xs + swiglu(rmsnorm(xs, norm_ffn[i]), w_ffn1[j], w_ffn3[j], w_ffn2[j])
        else:
            xp = xp + moe_ffn(rmsnorm(xp, norm_ffn[i]), w_router[j], w_exp1[j], w_exp3[j], w_exp2[j])
            xs = xs + moe_ffn(rmsnorm(xs, norm_ffn[i]), w_router[j], w_exp1[j], w_exp3[j], w_exp2[j])
    return (xp, xs,
            jnp.stack(p_conv_a), jnp.stack(p_conv_s), jnp.stack(p_ssm),
            jnp.stack(p_ckv), jnp.stack(p_kr), jnp.stack(p_mk), jnp.stack(p_mv),
            jnp.stack(s_conv_a), jnp.stack(s_conv_s), jnp.stack(s_ssm),
            jnp.stack(s_ckv), jnp.stack(s_kr))
```

```python
import functools
import math

import jax
import jax.numpy as jnp
from jax import lax
from jax.experimental import pallas as pl
from jax.experimental.pallas import tpu as pltpu

F32 = jnp.float32
BF16 = jnp.bfloat16
EPS = 1e-6

D_MODEL = 1024
CHUNK = 64
A_WIDTH = 1024
SSM_HEADS = 32
SSM_HEAD_DIM = 64
SSM_INNER = 2048
SSM_STATE = 128
SSM_GROUPS = 4
SSM_XBC = 3072
HEAD_LANES = 128
PROJ_W = 3 * A_WIDTH + SSM_INNER + SSM_XBC + HEAD_LANES
COL_Z = 3 * A_WIDTH
COL_XBC = COL_Z + SSM_INNER
COL_DT = COL_XBC + SSM_XBC
MLA_HEADS = 16
MLA_Q_LORA = 512
MLA_KV_LORA = 256
MLA_NOPE = 64
MLA_ROPE = 32
MLA_V = 64
MLA_SCALE = (MLA_NOPE + MLA_ROPE) ** -0.5
ROPE_THETA = 10000.0
MEM_TOKENS = 256
MEM_HEADS = 4
MEM_HEAD_DIM = 256
N_EXPERTS = 8
VMEM_LIMIT = 56 * 1024 * 1024

NT_DIMS = (((1,), (1,)), ((), ()))


def _params(*sem):
    return pltpu.CompilerParams(dimension_semantics=sem, vmem_limit_bytes=VMEM_LIMIT)


def _rms(x, g):
    return x * lax.rsqrt(jnp.mean(x * x, axis=-1, keepdims=True) + EPS) * g


def _split3(x):
    hi = x.astype(BF16)
    r1 = x - hi.astype(F32)
    mid = r1.astype(BF16)
    lo = (r1 - mid.astype(F32)).astype(BF16)
    return hi, mid, lo


def _dot_sel(x, sel_bf16):
    hi, mid, lo = _split3(x)
    d = lambda a: jnp.dot(a, sel_bf16, preferred_element_type=F32)
    return d(hi) + d(mid) + d(lo)


def _sel_dot_nt(sel_bf16, x):
    hi, mid, lo = _split3(x)
    d = lambda a: lax.dot_general(sel_bf16, a, NT_DIMS, preferred_element_type=F32)
    return d(hi) + d(mid) + d(lo)


def _sel_dot(sel_bf16, x):
    hi, mid, lo = _split3(x)
    d = lambda a: jnp.dot(sel_bf16, a, preferred_element_type=F32)
    return d(hi) + d(mid) + d(lo)


def _rms_matmul_body(*refs, use_norm, use_res):
    it = iter(refs)
    x_ref = next(it)
    g_ref = next(it) if use_norm else None
    w_ref = next(it)
    r_ref = next(it) if use_res else None
    o_ref = next(it)
    xn_ref = next(it) if use_norm else None
    if use_norm:
        @pl.when(pl.program_id(1) == 0)
        def _():
            xn_ref[...] = _rms(x_ref[...], g_ref[...]).astype(BF16)
        xb = xn_ref[...]
    else:
        xb = x_ref[...]
    acc = jnp.dot(xb, w_ref[...], preferred_element_type=F32)
    if use_res:
        acc = acc + r_ref[...]
    o_ref[...] = acc.astype(o_ref.dtype)


def rms_matmul(x, w, g=None, res=None, out_dtype=F32, tm=512, tn=None, name="rms_matmul"):
    M, K = x.shape
    N = w.shape[1]
    tn = tn or N
    tm = min(tm, M)
    in_specs = [pl.BlockSpec((tm, K), lambda i, j: (i, 0))]
    args = [x]
    if g is not None:
        in_specs.append(pl.BlockSpec((1, K), lambda i, j: (0, 0)))
        args.append(g.reshape(1, K))
    in_specs.append(pl.BlockSpec((K, tn), lambda i, j: (0, j)))
    args.append(w)
    if res is not None:
        in_specs.append(pl.BlockSpec((tm, tn), lambda i, j: (i, j)))
        args.append(res)
    scratch = [pltpu.VMEM((tm, K), BF16)] if g is not None else []
    return pl.pallas_call(
        functools.partial(_rms_matmul_body, use_norm=g is not None, use_res=res is not None),
        name=name,
        grid=(M // tm, N // tn),
        in_specs=in_specs,
        out_specs=pl.BlockSpec((tm, tn), lambda i, j: (i, j)),
        out_shape=jax.ShapeDtypeStruct((M, N), out_dtype),
        scratch_shapes=scratch,
        compiler_params=_params("parallel", "arbitrary"),
    )(*args)


def _softplus(v):
    return jnp.maximum(v, 0.0) + jnp.log1p(jnp.exp(-jnp.abs(v)))


def _mixer_body(proj_ref, wa_ref, ws_ref, bs_ref, dtb_ref, alog_ref, dskip_ref, gn_ref,
                tri_ref, selrow_ref, expand_ref, eye_ref,
                ca_prev_ref, cs_prev_ref, st_prev_ref,
                mix_ref, ca_out_ref, cs_out_ref, st_ref,
                abuf, sbuf):
    c = pl.program_id(1)
    Q = CHUNK

    @pl.when(c == 0)
    def _():
        abuf[0:8, :] = ca_prev_ref[0]
        sbuf[0:8, :] = cs_prev_ref[0]
        st_ref[0] = st_prev_ref[0]

    g_b = proj_ref[0, :, 0:A_WIDTH]
    abuf[8:8 + Q, :] = proj_ref[0, :, A_WIDTH:2 * A_WIDTH] * proj_ref[0, :, 2 * A_WIDTH:3 * A_WIDTH]
    conv_a = wa_ref[0:1, :] * abuf[6:6 + Q, :]
    conv_a = conv_a + wa_ref[1:2, :] * abuf[7:7 + Q, :]
    conv_a = conv_a + wa_ref[2:3, :] * abuf[8:8 + Q, :]
    mix_ref[0, :, 0:A_WIDTH] = (g_b * conv_a).astype(BF16)
    tail_a = abuf[Q:Q + 8, :]
    ca_out_ref[0] = tail_a
    abuf[0:8, :] = tail_a

    sbuf[8:8 + Q, :] = proj_ref[0, :, COL_XBC:COL_DT]
    xc = ws_ref[0:1, :] * sbuf[5:5 + Q, :]
    xc = xc + ws_ref[1:2, :] * sbuf[6:6 + Q, :]
    xc = xc + ws_ref[2:3, :] * sbuf[7:7 + Q, :]
    xc = xc + ws_ref[3:4, :] * sbuf[8:8 + Q, :]
    xc = xc + bs_ref[...]
    xc = xc * jax.nn.sigmoid(xc)
    tail_s = sbuf[Q:Q + 8, :]
    cs_out_ref[0] = tail_s
    sbuf[0:8, :] = tail_s

    dt = _softplus(proj_ref[0, :, COL_DT:COL_DT + HEAD_LANES] + dtb_ref[...])
    la = dt * (-jnp.exp(alog_ref[...]))
    cs = _sel_dot(tri_ref[...], la)
    cs_last = cs[Q - 1:Q, :]
    e_in = jnp.exp(cs)
    w_st = dt * jnp.exp(cs_last - cs)
    dec = jnp.broadcast_to(jnp.exp(cs_last), (8, HEAD_LANES))
    ex = _dot_sel(jnp.concatenate([cs, e_in, w_st, dec], axis=0), expand_ref[...])
    cs_x, e_in_x, w_st_x, dec_x = ex[0:Q], ex[Q:2 * Q], ex[2 * Q:3 * Q], ex[3 * Q:3 * Q + 1]
    cs_r = _sel_dot_nt(selrow_ref[...], cs)
    dt_r = _sel_dot_nt(selrow_ref[...], dt)
    cs_rp = jnp.concatenate([cs_r[0:16], cs_r[16:32]], axis=1)
    dt_rp = jnp.concatenate([dt_r[0:16], dt_r[16:32]], axis=1)

    lane = lax.broadcasted_iota(jnp.int32, (Q, 128), 1)
    row = lax.broadcasted_iota(jnp.int32, (Q, 128), 0)
    tril2 = jnp.where(lane >= Q, lane - Q, lane) <= row
    left = lane < Q
    eye = eye_ref[...]

    for g in range(SSM_GROUPS):
        gl = slice(SSM_INNER + g * SSM_STATE, SSM_INNER + (g + 1) * SSM_STATE)
        Bg = xc[:, gl].astype(BF16)
        Cg = xc[:, gl.start + SSM_GROUPS * SSM_STATE: gl.stop + SSM_GROUPS * SSM_STATE].astype(BF16)
        sc = lax.dot_general(Cg, Bg, NT_DIMS, preferred_element_type=F32)
        sc2 = jnp.concatenate([sc, sc], axis=1)
        BgT = lax.dot_general(eye, Bg, NT_DIMS, preferred_element_type=F32).astype(BF16)
        hs = slice(g * 512, (g + 1) * 512)
        st_g = st_ref[0, :, hs]
        y_off = jnp.dot(Cg, st_g.astype(BF16), preferred_element_type=F32)
        xg = xc[:, hs]
        xw = (xg * w_st_x[:, hs]).astype(BF16)
        st_ref[0, :, hs] = dec_x[:, hs] * st_g + jnp.dot(BgT, xw, preferred_element_type=F32)
        ys = []
        for kk in range(4):
            k = 4 * g + kk
            pl_ = slice(k * 128, (k + 1) * 128)
            diff = cs_x[:, pl_] - cs_rp[k:k + 1, :]
            decay = jnp.exp(jnp.where(tril2, diff, -jnp.inf))
            m_pair = (sc2 * decay * dt_rp[k:k + 1, :]).astype(BF16)
            xp = xc[:, pl_]
            rhs = jnp.concatenate([jnp.where(left, xp, 0.0), jnp.where(left, 0.0, xp)], axis=0).astype(BF16)
            y_diag = jnp.dot(m_pair, rhs, preferred_element_type=F32)
            y = y_diag + y_off[:, kk * 128:(kk + 1) * 128] * e_in_x[:, pl_]
            y = y + dskip_ref[:, pl_] * xp
            z = proj_ref[0, :, COL_Z + k * 128: COL_Z + (k + 1) * 128]
            ys.append(y * (z * jax.nn.sigmoid(z)))
        yg = jnp.concatenate(ys, axis=1)
        yn = yg * lax.rsqrt(jnp.mean(yg * yg, axis=-1, keepdims=True) + EPS) * gn_ref[:, hs]
        mix_ref[0, :, A_WIDTH + g * 512: A_WIDTH + (g + 1) * 512] = yn.astype(BF16)


def mixer_even(proj, conv_a_w, conv_s_w, conv_s_b, dt_bias, a_log, d_skip, ssm_norm,
               conv_a_prev8, conv_s_prev8, ssm_prev_t):
    B, L, _ = proj.shape
    nc = L // CHUNK
    pad_h = lambda v: jnp.pad(v.reshape(1, SSM_HEADS), ((0, 0), (0, HEAD_LANES - SSM_HEADS)))
    tri = (jnp.arange(CHUNK)[:, None] >= jnp.arange(CHUNK)[None, :]).astype(BF16)
    heads = jnp.arange(HEAD_LANES)
    order = jnp.concatenate([jnp.arange(0, SSM_HEADS, 2), jnp.arange(1, SSM_HEADS, 2)])
    selrow = (order[:, None] == heads[None, :]).astype(BF16)
    expand = (heads[:, None] == (jnp.arange(SSM_INNER) // SSM_HEAD_DIM)[None, :]).astype(BF16)
    eye = jnp.eye(SSM_STATE, dtype=BF16)
    dskip_x = jnp.repeat(d_skip.astype(F32), SSM_HEAD_DIM).reshape(1, SSM_INNER)
    const = lambda shape: pl.BlockSpec(shape, lambda b, c: (0,) * len(shape))
    per_b = lambda shape: pl.BlockSpec((1,) + shape, lambda b, c: (b,) + (0,) * len(shape))
    return pl.pallas_call(
        _mixer_body,
        name="ssd_mixer",
        grid=(B, nc),
        in_specs=[
            pl.BlockSpec((1, CHUNK, PROJ_W), lambda b, c: (b, c, 0)),
            const((3, A_WIDTH)), const((4, SSM_XBC)), const((1, SSM_XBC)),
            const((1, HEAD_LANES)), const((1, HEAD_LANES)), const((1, SSM_INNER)), const((1, SSM_INNER)),
            const((CHUNK, CHUNK)), const((SSM_HEADS, HEAD_LANES)), const((HEAD_LANES, SSM_INNER)),
            const((SSM_STATE, SSM_STATE)),
            per_b((8, A_WIDTH)), per_b((8, SSM_XBC)), per_b((SSM_STATE, SSM_INNER)),
        ],
        out_specs=[
            pl.BlockSpec((1, CHUNK, A_WIDTH + SSM_INNER), lambda b, c: (b, c, 0)),
            per_b((8, A_WIDTH)), per_b((8, SSM_XBC)), per_b((SSM_STATE, SSM_INNER)),
        ],
        out_shape=[
            jax.ShapeDtypeStruct((B, L, A_WIDTH + SSM_INNER), BF16),
            jax.ShapeDtypeStruct((B, 8, A_WIDTH), F32),
            jax.ShapeDtypeStruct((B, 8, SSM_XBC), F32),
            jax.ShapeDtypeStruct((B, SSM_STATE, SSM_INNER), F32),
        ],
        scratch_shapes=[
            pltpu.VMEM((CHUNK + 8, A_WIDTH), F32),
            pltpu.VMEM((CHUNK + 8, SSM_XBC), F32),
        ],
        compiler_params=_params("parallel", "arbitrary"),
    )(proj, conv_a_w, conv_s_w, conv_s_b.reshape(1, SSM_XBC), pad_h(dt_bias), pad_h(a_log), dskip_x,
      ssm_norm.reshape(1, SSM_INNER), tri, selrow, expand, eye, conv_a_prev8, conv_s_prev8, ssm_prev_t)


def even_weights(w_in, w_out, norm_g, conv_a_w, conv_s_w, conv_s_b, dt_bias, a_log, d_skip, ssm_norm):
    w_in_pad = jnp.pad(w_in, ((0, 0), (0, PROJ_W - w_in.shape[1]))).astype(BF16)
    return (norm_g, w_in_pad, w_out.astype(BF16), (conv_a_w, conv_s_w, conv_s_b, dt_bias, a_log, d_skip, ssm_norm))


def even_mixer_block(x, b, l, ew, ca_prev, cs_prev, st_prev, tm):
    norm_g, w_in_pad, w_out, mixer_w = ew
    proj = rms_matmul(x, w_in_pad, g=norm_g, tm=tm, tn=PROJ_W // 5, name="mixer_in_proj")
    ca8 = jnp.pad(ca_prev, ((0, 0), (6, 0), (0, 0)))
    cs8 = jnp.pad(cs_prev, ((0, 0), (5, 0), (0, 0)))
    st_t = jnp.transpose(st_prev, (0, 3, 1, 2)).reshape(b, SSM_STATE, SSM_INNER)
    mix, ca, cs, st = mixer_even(proj.reshape(b, l, PROJ_W), *mixer_w, ca8, cs8, st_t)
    x = rms_matmul(mix.reshape(b * l, -1), w_out, res=x, tm=tm, name="mixer_out_proj")
    st = jnp.transpose(st.reshape(b, SSM_STATE, SSM_HEADS, SSM_HEAD_DIM), (0, 2, 3, 1))
    return x, ca[:, 6:8], cs[:, 5:8], st


def _mem_kv_body(mem_ref, g_ref, wk_ref, wv_ref, kn_ref, k_ref, v_ref, kb_ref, vb_ref):
    mm = _rms(mem_ref[0], g_ref[...]).astype(BF16)
    k = jnp.dot(mm, wk_ref[...], preferred_element_type=F32)
    v = jnp.dot(mm, wv_ref[...], preferred_element_type=F32)
    for h in range(MEM_HEADS):
        sl = slice(h * MEM_HEAD_DIM, (h + 1) * MEM_HEAD_DIM)
        kh = _rms(k[:, sl], kn_ref[...])
        k_ref[0, :, sl] = kh
        kb_ref[0, :, sl] = kh.astype(BF16)
    v_ref[0] = v
    vb_ref[0] = v.astype(BF16)


def mem_kv(mem, src_norm, w_k, w_v, k_norm):
    B, Mt, D = mem.shape
    const = lambda shape: pl.BlockSpec(shape, lambda b: (0,) * len(shape))
    blk = pl.BlockSpec((1, Mt, D), lambda b: (b, 0, 0))
    return pl.pallas_call(
        _mem_kv_body,
        name="mem_kv",
        grid=(B,),
        in_specs=[blk, const((1, D)), const((D, D)), const((D, D)), const((1, MEM_HEAD_DIM))],
        out_specs=[blk, blk, blk, blk],
        out_shape=[jax.ShapeDtypeStruct((B, Mt, D), F32), jax.ShapeDtypeStruct((B, Mt, D), F32),
                   jax.ShapeDtypeStruct((B, Mt, D), BF16), jax.ShapeDtypeStruct((B, Mt, D), BF16)],
        compiler_params=_params("parallel"),
    )(mem, src_norm.reshape(1, D), w_k, w_v, k_norm.reshape(1, MEM_HEAD_DIM))


def _mem_attn_body(x_ref, g_ref, wq_ref, qn_ref, k_ref, v_ref, wo_ref, o_ref, *, nb):
    x = x_ref[...]
    tm = x.shape[0]
    rows = tm // nb
    hn = _rms(x, g_ref[...]).astype(BF16)
    q = jnp.dot(hn, wq_ref[...], preferred_element_type=F32)
    scale = MEM_HEAD_DIM ** -0.5
    outs = []
    for h in range(MEM_HEADS):
        sl = slice(h * MEM_HEAD_DIM, (h + 1) * MEM_HEAD_DIM)
        qh = _rms(q[:, sl], qn_ref[...]).astype(BF16)
        parts = []
        for s in range(nb):
            qs = qh[s * rows:(s + 1) * rows]
            sc = lax.dot_general(qs, k_ref[s, :, sl], NT_DIMS, preferred_element_type=F32) * scale
            sc = sc - jnp.max(sc, axis=-1, keepdims=True)
            p = jnp.exp(sc)
            p = p / jnp.sum(p, axis=-1, keepdims=True)
            parts.append(jnp.dot(p.astype(BF16), v_ref[s, :, sl], preferred_element_type=F32))
        outs.append(parts[0] if nb == 1 else jnp.concatenate(parts, axis=0))
    o = jnp.concatenate(outs, axis=1).astype(BF16)
    o_ref[...] = x + jnp.dot(o, wo_ref[...], preferred_element_type=F32)


def mem_attend(x, norm_g, k_b, v_b, w_q, q_norm, w_o, rows_per_seq, tm=512):
    M, D = x.shape
    tm = min(tm, M)
    nb = max(1, tm // rows_per_seq)
    tiles_per_seq = max(1, rows_per_seq // tm)
    const = lambda shape: pl.BlockSpec(shape, lambda i: (0,) * len(shape))
    kv_spec = pl.BlockSpec((nb, MEM_TOKENS, D), lambda i: (i // tiles_per_seq, 0, 0))
    return pl.pallas_call(
        functools.partial(_mem_attn_body, nb=nb),
        name="mem_attn",
        grid=(M // tm,),
        in_specs=[pl.BlockSpec((tm, D), lambda i: (i, 0)), const((1, D)), const((D, D)),
                  const((1, MEM_HEAD_DIM)), kv_spec, kv_spec, const((D, D))],
        out_specs=pl.BlockSpec((tm, D), lambda i: (i, 0)),
        out_shape=jax.ShapeDtypeStruct((M, D), F32),
        compiler_params=_params("parallel"),
    )(x, norm_g.reshape(1, D), w_q, q_norm.reshape(1, MEM_HEAD_DIM), k_b, v_b, w_o)


def _swiglu_step(xn, w1, w3, w2):
    h1 = jnp.dot(xn, w1, preferred_element_type=F32)
    h3 = jnp.dot(xn, w3, preferred_element_type=F32)
    a = (h1 * jax.nn.sigmoid(h1) * h3).astype(BF16)
    return jnp.dot(a, w2, preferred_element_type=F32)


def _ffn_dense_body(x_ref, g_ref, w1_ref, w3_ref, w2_ref, o_ref, xn_ref, acc_ref):
    f = pl.program_id(1)

    @pl.when(f == 0)
    def _():
        xn_ref[...] = _rms(x_ref[...], g_ref[...]).astype(BF16)
        acc_ref[...] = jnp.zeros_like(acc_ref)

    acc_ref[...] += _swiglu_step(xn_ref[...], w1_ref[...], w3_ref[...], w2_ref[...])

    @pl.when(f == pl.num_programs(1) - 1)
    def _():
        o_ref[...] = x_ref[...] + acc_ref[...]


def ffn_dense(x, norm_g, w1, w3, w2, tm=512, tf=1408):
    M, D = x.shape
    F = w1.shape[1]
    tm = min(tm, M)
    return pl.pallas_call(
        _ffn_dense_body,
        name="ffn_dense",
        grid=(M // tm, F // tf),
        in_specs=[pl.BlockSpec((tm, D), lambda i, f: (i, 0)), pl.BlockSpec((1, D), lambda i, f: (0, 0)),
                  pl.BlockSpec((D, tf), lambda i, f: (0, f)), pl.BlockSpec((D, tf), lambda i, f: (0, f)),
                  pl.BlockSpec((tf, D), lambda i, f: (f, 0))],
        out_specs=pl.BlockSpec((tm, D), lambda i, f: (i, 0)),
        out_shape=jax.ShapeDtypeStruct((M, D), F32),
        scratch_shapes=[pltpu.VMEM((tm, D), BF16), pltpu.VMEM((tm, D), F32)],
        compiler_params=_params("parallel", "arbitrary"),
    )(x, norm_g.reshape(1, D), w1, w3, w2)


def _ffn_expert_body(te_ref, tv_ref, x_ref, gate_ref, w1_ref, w3_ref, w2_ref, o_ref, xn_ref, acc_ref):
    i = pl.program_id(0)
    f = pl.program_id(1)
    last = f == pl.num_programs(1) - 1

    @pl.when(tv_ref[i] > 0)
    def _():
        @pl.when(f == 0)
        def _():
            xn_ref[...] = x_ref[...].astype(BF16)
            acc_ref[...] = jnp.zeros_like(acc_ref)

        acc_ref[...] += _swiglu_step(xn_ref[...], w1_ref[0], w3_ref[0], w2_ref[0])

        @pl.when(last)
        def _():
            o_ref[...] = gate_ref[:, 0:1] * acc_ref[...]

    @pl.when(jnp.logical_and(tv_ref[i] == 0, last))
    def _():
        o_ref[...] = jnp.zeros_like(o_ref)


def ffn_experts(xs, gate_rows, tile_expert, tile_valid, w1, w3, w2, tm, tf=1792):
    N, D = xs.shape
    F = w1.shape[2]
    grid_spec = pltpu.PrefetchScalarGridSpec(
        num_scalar_prefetch=2,
        grid=(N // tm, F // tf),
        in_specs=[pl.BlockSpec((tm, D), lambda i, f, te, tv: (i, 0)),
                  pl.BlockSpec((tm, 128), lambda i, f, te, tv: (i, 0)),
                  pl.BlockSpec((1, D, tf), lambda i, f, te, tv: (te[i], 0, f)),
                  pl.BlockSpec((1, D, tf), lambda i, f, te, tv: (te[i], 0, f)),
                  pl.BlockSpec((1, tf, D), lambda i, f, te, tv: (te[i], f, 0))],
        out_specs=pl.BlockSpec((tm, D), lambda i, f, te, tv: (i, 0)),
        scratch_shapes=[pltpu.VMEM((tm, D), BF16), pltpu.VMEM((tm, D), F32)],
    )
    return pl.pallas_call(
        _ffn_expert_body,
        name="ffn_experts",
        grid_spec=grid_spec,
        out_shape=jax.ShapeDtypeStruct((N, D), F32),
        compiler_params=_params("arbitrary", "arbitrary"),
    )(tile_expert, tile_valid, xs, gate_rows, w1, w3, w2)


def _router_body(x_ref, g_ref, wr_ref, hn_ref, route_ref):
    hn = _rms(x_ref[...], g_ref[...])
    hn_ref[...] = hn
    xh, xm, xl = _split3(hn)
    wh, wm, wl = wr_ref[0], wr_ref[1], wr_ref[2]
    d = lambda a, b: jnp.dot(a, b, preferred_element_type=F32)
    logits = d(xh, wh) + (d(xh, wm) + d(xm, wh)) + (d(xm, wm) + d(xh, wl) + d(xl, wh))
    lane = lax.broadcasted_iota(jnp.int32, logits.shape, 1)
    lane_f = lane.astype(F32)
    neg = -jnp.inf
    lg = jnp.where(lane < N_EXPERTS, logits, neg)
    m1 = jnp.max(lg, axis=-1, keepdims=True)
    i1 = jnp.min(jnp.where(lg == m1, lane_f, 128.0), axis=-1, keepdims=True)
    lg2 = jnp.where(lane_f == i1, neg, lg)
    m2 = jnp.max(lg2, axis=-1, keepdims=True)
    i2 = jnp.min(jnp.where(lg2 == m2, lane_f, 128.0), axis=-1, keepdims=True)
    e = jnp.exp(m2 - m1)
    den = 1.0 + e
    g1 = 1.0 / den
    g2 = e / den
    route = jnp.where(lane == 0, g1, jnp.where(lane == 1, g2, jnp.where(lane == 2, i1, jnp.where(lane == 3, i2, 0.0))))
    route_ref[...] = route


def router(x, norm_g, w_router, tm=512):
    M, D = x.shape
    tm = min(tm, M)
    wr = jnp.pad(w_router.astype(F32), ((0, 0), (0, 128 - N_EXPERTS)))
    wh = wr.astype(BF16)
    r1 = wr - wh.astype(F32)
    wm = r1.astype(BF16)
    wl = (r1 - wm.astype(F32)).astype(BF16)
    w3 = jnp.stack([wh, wm, wl])
    return pl.pallas_call(
        _router_body,
        name="router",
        grid=(M // tm,),
        in_specs=[pl.BlockSpec((tm, D), lambda i: (i, 0)), pl.BlockSpec((1, D), lambda i: (0, 0)),
                  pl.BlockSpec((3, D, 128), lambda i: (0, 0, 0))],
        out_specs=[pl.BlockSpec((tm, D), lambda i: (i, 0)), pl.BlockSpec((tm, 128), lambda i: (i, 0))],
        out_shape=[jax.ShapeDtypeStruct((M, D), F32), jax.ShapeDtypeStruct((M, 128), F32)],
        compiler_params=_params("parallel"),
    )(x, norm_g.reshape(1, D), w3)


def _row_copy(src_hbm, idx, dst_ref, r, sem):
    return pltpu.make_async_copy(src_hbm.at[pl.ds(idx, 1), :], dst_ref.at[pl.ds(r, 1), :], sem)


def _gather_rows_body(idx_ref, src_hbm, o_ref, sem):
    tm = o_ref.shape[0]

    def start(r, _):
        _row_copy(src_hbm, idx_ref[0, 0, r], o_ref, r, sem).start()
        return 0

    lax.fori_loop(0, tm, start, 0)

    def wait(r, _):
        _row_copy(src_hbm, 0, o_ref, r, sem).wait()
        return 0

    lax.fori_loop(0, tm, wait, 0)


def gather_rows(src, idx, tm):
    N = idx.shape[0]
    D = src.shape[1]
    return pl.pallas_call(
        _gather_rows_body,
        name="gather_rows",
        grid=(N // tm,),
        in_specs=[pl.BlockSpec((1, 1, tm), lambda i: (i, 0, 0), memory_space=pltpu.SMEM),
                  pl.BlockSpec(memory_space=pl.ANY)],
        out_specs=pl.BlockSpec((tm, D), lambda i: (i, 0)),
        out_shape=jax.ShapeDtypeStruct((N, D), src.dtype),
        scratch_shapes=[pltpu.SemaphoreType.DMA(())],
        compiler_params=_params("arbitrary"),
    )(idx.reshape(N // tm, 1, tm), src)


def _combine_body(i1_ref, i2_ref, x_ref, ys_hbm, o_ref, b1, b2, sem1, sem2):
    tm = o_ref.shape[0]

    def start(r, _):
        _row_copy(ys_hbm, i1_ref[0, 0, r], b1, r, sem1).start()
        _row_copy(ys_hbm, i2_ref[0, 0, r], b2, r, sem2).start()
        return 0

    lax.fori_loop(0, tm, start, 0)

    def wait(r, _):
        _row_copy(ys_hbm, 0, b1, r, sem1).wait()
        _row_copy(ys_hbm, 0, b2, r, sem2).wait()
        return 0

    lax.fori_loop(0, tm, wait, 0)
    o_ref[...] = x_ref[...] + b1[...] + b2[...]


def combine_rows(x, ys, pos1, pos2, tm=256):
    M, D = x.shape
    tm = min(tm, M)
    idx_spec = pl.BlockSpec((1, 1, tm), lambda i: (i, 0, 0), memory_space=pltpu.SMEM)
    return pl.pallas_call(
        _combine_body,
        name="combine_rows",
        grid=(M // tm,),
        in_specs=[idx_spec, idx_spec, pl.BlockSpec((tm, D), lambda i: (i, 0)),
                  pl.BlockSpec(memory_space=pl.ANY)],
        out_specs=pl.BlockSpec((tm, D), lambda i: (i, 0)),
        out_shape=jax.ShapeDtypeStruct((M, D), F32),
        scratch_shapes=[pltpu.VMEM((tm, D), F32), pltpu.VMEM((tm, D), F32),
                        pltpu.SemaphoreType.DMA(()), pltpu.SemaphoreType.DMA(())],
        compiler_params=_params("arbitrary"),
    )(pos1.reshape(M // tm, 1, tm), pos2.reshape(M // tm, 1, tm), x, ys)


def moe_ffn(x, norm_g, w_router, w1, w3, w2, tm_e):
    T, D = x.shape
    hn, route = router(x, norm_g, w_router)
    top_g = route[:, 0:2]
    top_i = route[:, 2:4].astype(jnp.int32)
    e_flat = top_i.reshape(-1)
    onehot = (e_flat[:, None] == jnp.arange(N_EXPERTS, dtype=jnp.int32)[None, :]).astype(jnp.int32)
    csum = jnp.cumsum(onehot, axis=0)
    counts = csum[-1]
    rank = jnp.take_along_axis(csum, e_flat[:, None], axis=1)[:, 0] - 1
    ptiles = (counts + tm_e - 1) // tm_e
    tile_end = jnp.cumsum(ptiles)
    pos = (tile_end - ptiles)[e_flat] * tm_e + rank
    n_tiles = (2 * T) // tm_e + N_EXPERTS
    n_rows = n_tiles * tm_e
    token = jnp.arange(2 * T, dtype=jnp.int32) // 2
    src = jnp.zeros((n_rows,), jnp.int32).at[pos].set(token, unique_indices=True)
    gate_sorted = jnp.zeros((n_rows,), F32).at[pos].set(top_g.reshape(-1), unique_indices=True)
    tile_ids = jnp.arange(n_tiles, dtype=jnp.int32)
    tile_valid = (tile_ids < tile_end[-1]).astype(jnp.int32)
    last_e = jnp.max(jnp.where(counts > 0, jnp.arange(N_EXPERTS, dtype=jnp.int32), 0))
    tile_expert = jnp.sum((tile_end[None, :] <= tile_ids[:, None]).astype(jnp.int32), axis=1)
    tile_expert = jnp.minimum(tile_expert, last_e)
    xs = gather_rows(hn, src, tm_e)
    gate_rows = jnp.broadcast_to(gate_sorted[:, None], (n_rows, 128))
    ys = ffn_experts(xs, gate_rows, tile_expert, tile_valid, w1, w3, w2, tm_e)
    return combine_rows(x, ys, pos[0::2], pos[1::2])


def _rope_lanes(v, cos, sin, lane):
    partner = jnp.where(lane < 80, pltpu.roll(v, 112, 1), pltpu.roll(v, 16, 1))
    return v * cos + partner * sin


def _mla_rows_body(x_ref, g_ref, wd_ref, qln_ref, kvln_ref, krn_ref, wuq_ref, qn_ref, cos_ref, sin_ref,
                   q_ref, ckv_ref, kr_ref):
    hn = _rms(x_ref[...], g_ref[...]).astype(BF16)
    d = jnp.dot(hn, wd_ref[...], preferred_element_type=F32)
    cq = _rms(d[:, 0:MLA_Q_LORA], qln_ref[...]).astype(BF16)
    ckv_ref[...] = _rms(d[:, MLA_Q_LORA:MLA_Q_LORA + MLA_KV_LORA], kvln_ref[...])
    cos = cos_ref[...]
    sin = sin_ref[...]
    lane = lax.broadcasted_iota(jnp.int32, cos.shape, 1)
    kr = d[:, MLA_Q_LORA + MLA_KV_LORA:]
    kr = kr * lax.rsqrt(jnp.sum(kr * kr, axis=-1, keepdims=True) * (1.0 / MLA_ROPE) + EPS) * krn_ref[...]
    kr_ref[...] = _rope_lanes(kr, cos, sin, lane)
    q = jnp.dot(cq, wuq_ref[...], preferred_element_type=F32)
    nope = lane < MLA_NOPE
    for h in range(MLA_HEADS):
        qh = q[:, h * 128:(h + 1) * 128]
        sq = qh * qh
        ss_n = jnp.sum(jnp.where(nope, sq, 0.0), axis=-1, keepdims=True) * (1.0 / MLA_NOPE)
        ss_r = jnp.sum(jnp.where(nope, 0.0, sq), axis=-1, keepdims=True) * (1.0 / MLA_ROPE)
        r = jnp.where(nope, lax.rsqrt(ss_n + EPS), lax.rsqrt(ss_r + EPS))
        q_ref[:, h * 128:(h + 1) * 128] = _rope_lanes(qh * r * qn_ref[...], cos, sin, lane).astype(BF16)


def mla_rows(x, norm_g, wd_pad, q_lora_norm, kv_lora_norm, krn_pad, wuq_pad, qn_pad, cos_t, sin_t, tm=512):
    M, D = x.shape
    tm = min(tm, M)
    const = lambda shape: pl.BlockSpec(shape, lambda i: (0,) * len(shape))
    rows = lambda n: pl.BlockSpec((tm, n), lambda i: (i, 0))
    return pl.pallas_call(
        _mla_rows_body,
        name="mla_rows",
        grid=(M // tm,),
        in_specs=[rows(D), const((1, D)), const(wd_pad.shape), const((1, MLA_Q_LORA)), const((1, MLA_KV_LORA)),
                  const((1, 128)), const(wuq_pad.shape), const((1, 128)), rows(128), rows(128)],
        out_specs=[rows(MLA_HEADS * 128), rows(MLA_KV_LORA), rows(128)],
        out_shape=[jax.ShapeDtypeStruct((M, MLA_HEADS * 128), BF16), jax.ShapeDtypeStruct((M, MLA_KV_LORA), F32),
                   jax.ShapeDtypeStruct((M, 128), F32)],
        compiler_params=_params("parallel"),
    )(x, norm_g.reshape(1, D), wd_pad, q_lora_norm.reshape(1, -1), kv_lora_norm.reshape(1, -1), krn_pad,
      wuq_pad, qn_pad, cos_t, sin_t)


def _mla_kv_body(ckv_ref, kr_ref, wuk_ref, wuv_ref, kn_ref, k_ref, v_ref):
    c = ckv_ref[...].astype(BF16)
    kn = jnp.dot(c, wuk_ref[...], preferred_element_type=F32)
    v_ref[...] = jnp.dot(c, wuv_ref[...], preferred_element_type=F32).astype(BF16)
    kr = kr_ref[...]
    for h in range(MLA_HEADS):
        kh = kn[:, h * 128:(h + 1) * 128]
        r = lax.rsqrt(jnp.sum(kh * kh, axis=-1, keepdims=True) * (1.0 / MLA_NOPE) + EPS)
        k_ref[:, h * 128:(h + 1) * 128] = (kh * r * kn_ref[...] + kr).astype(BF16)


def mla_kv(ckv, kr_pad, wuk_pad, wuv, kn_pad, tm=512):
    M = ckv.shape[0]
    tm = min(tm, M)
    const = lambda shape: pl.BlockSpec(shape, lambda i: (0,) * len(shape))
    rows = lambda n: pl.BlockSpec((tm, n), lambda i: (i, 0))
    return pl.pallas_call(
        _mla_kv_body,
        name="mla_kv",
        grid=(M // tm,),
        in_specs=[rows(MLA_KV_LORA), rows(128), const(wuk_pad.shape), const(wuv.shape), const((1, 128))],
        out_specs=[rows(MLA_HEADS * 128), rows(MLA_HEADS * MLA_V)],
        out_shape=[jax.ShapeDtypeStruct((M, MLA_HEADS * 128), BF16),
                   jax.ShapeDtypeStruct((M, MLA_HEADS * MLA_V), BF16)],
        compiler_params=_params("parallel"),
    )(ckv, kr_pad, wuk_pad, wuv, kn_pad)


def _flash_body(q_ref, k_ref, v_ref, o_ref, m_ref, l_ref, acc_ref, *, causal):
    qi = pl.program_id(2)
    ki = pl.program_id(3)
    tq = q_ref.shape[1]
    tk = k_ref.shape[1]
    c2 = MLA_SCALE * math.log2(math.e)

    @pl.when(ki == 0)
    def _():
        m_ref[...] = jnp.full_like(m_ref, -jnp.inf)
        l_ref[...] = jnp.zeros_like(l_ref)
        acc_ref[...] = jnp.zeros_like(acc_ref)

    def step(masked):
        if masked:
            shift = CHUNK.bit_length() - 1
            rq = lax.shift_right_logical(lax.broadcasted_iota(jnp.int32, (tq, tk), 0), shift)
            ck = lax.shift_right_logical(lax.broadcasted_iota(jnp.int32, (tq, tk), 1), shift)
            keep = ck <= rq
        for h in range(2):
            q = q_ref[0, :, h * 128:(h + 1) * 128]
            k = k_ref[0, :, h * 128:(h + 1) * 128]
            s = lax.dot_general(q, k, NT_DIMS, preferred_element_type=F32)
            if masked:
                s = jnp.where(keep, s, -jnp.inf)
            m_prev = m_ref[h]
            m_new = jnp.maximum(m_prev, jnp.max(s, axis=-1, keepdims=True))
            alpha = jnp.exp2((m_prev - m_new) * c2)
            p = jnp.exp2((s - m_new) * c2)
            l_ref[h] = alpha * l_ref[h] + jnp.sum(p, axis=-1, keepdims=True)
            acc_ref[h] = alpha * acc_ref[h] + jnp.dot(p.astype(BF16), v_ref[0], preferred_element_type=F32)
            m_ref[h] = m_new

    def finish():
        lane = lax.broadcasted_iota(jnp.int32, (tq, 128), 1)
        o = jnp.where(lane < MLA_V, acc_ref[0] / l_ref[0], acc_ref[1] / l_ref[1])
        o_ref[0] = o.astype(o_ref.dtype)

    if causal:
        @pl.when(ki < qi)
        def _():
            step(False)

        @pl.when(ki == qi)
        def _():
            step(True)
            finish()
    else:
        step(False)

        @pl.when(ki == pl.num_programs(3) - 1)
        def _():
            finish()


def mla_attention(q, k, v, causal, tq, tk):
    B, Lq, _ = q.shape
    Lk = k.shape[1]
    kv_idx = (lambda b, p, i, j: (b, jnp.minimum(i, j), p)) if causal else (lambda b, p, i, j: (b, j, p))
    return pl.pallas_call(
        functools.partial(_flash_body, causal=causal),
        name="mla_flash",
        grid=(B, MLA_HEADS // 2, Lq // tq, Lk // tk),
        in_specs=[pl.BlockSpec((1, tq, 256), lambda b, p, i, j: (b, i, p)),
                  pl.BlockSpec((1, tk, 256), kv_idx),
                  pl.BlockSpec((1, tk, 128), kv_idx)],
        out_specs=pl.BlockSpec((1, tq, 128), lambda b, p, i, j: (b, i, p)),
        out_shape=jax.ShapeDtypeStruct((B, Lq, MLA_HEADS * MLA_V), BF16),
        scratch_shapes=[pltpu.VMEM((2, tq, 1), F32), pltpu.VMEM((2, tq, 1), F32), pltpu.VMEM((2, tq, 128), F32)],
        compiler_params=_params("parallel", "parallel", "parallel", "arbitrary"),
    )(q, k, v)


def _rope_tables(pos):
    half = MLA_ROPE // 2
    inv = ROPE_THETA ** (-jnp.arange(half, dtype=F32) / half)
    ang = pos.astype(F32)[:, None] * inv[None, :]
    cos, sin = jnp.cos(ang), jnp.sin(ang)
    n = pos.shape[0]
    ones = jnp.ones((n, MLA_NOPE), F32)
    zeros = jnp.zeros((n, 128 - MLA_NOPE - MLA_ROPE), F32)
    cos_t = jnp.concatenate([ones, cos, cos, zeros], axis=1)
    sin_t = jnp.concatenate([jnp.zeros((n, MLA_NOPE), F32), -sin, sin, zeros], axis=1)
    return cos_t, sin_t


def _pad_heads(w, d_in, d_head, d_pad):
    w = w.reshape(d_in, MLA_HEADS, d_head)
    return jnp.pad(w, ((0, 0), (0, 0), (0, d_pad - d_head))).reshape(d_in, MLA_HEADS * d_pad)


def mla_weights(norm_g, w_down, q_lora_norm, kv_lora_norm, w_uq, w_ukv, q_nope_norm, q_rope_norm, k_nope_norm,
                k_rope_norm, w_o):
    D = w_down.shape[0]
    split = MLA_Q_LORA + MLA_KV_LORA
    tail = 128 - MLA_NOPE - MLA_ROPE
    wd_pad = jnp.concatenate([w_down[:, :split], jnp.zeros((D, MLA_NOPE), F32), w_down[:, split:],
                              jnp.zeros((D, tail), F32)], axis=1).astype(BF16)
    lane_pad = lambda a, b: jnp.concatenate([a, b, jnp.zeros((128 - a.shape[0] - b.shape[0],), F32)]).reshape(1, 128)
    krn_pad = lane_pad(jnp.zeros((MLA_NOPE,), F32), k_rope_norm)
    qn_pad = lane_pad(q_nope_norm, q_rope_norm)
    kn_pad = lane_pad(k_nope_norm, jnp.zeros((0,), F32))
    wuq_pad = _pad_heads(w_uq, MLA_Q_LORA, MLA_NOPE + MLA_ROPE, 128).astype(BF16)
    wukv = w_ukv.reshape(MLA_KV_LORA, MLA_HEADS, MLA_NOPE + MLA_V)
    wuk_pad = _pad_heads(wukv[:, :, :MLA_NOPE].reshape(MLA_KV_LORA, -1), MLA_KV_LORA, MLA_NOPE, 128).astype(BF16)
    wuv = wukv[:, :, MLA_NOPE:].reshape(MLA_KV_LORA, MLA_HEADS * MLA_V).astype(BF16)
    rows_w = (norm_g, wd_pad, q_lora_norm, kv_lora_norm, krn_pad, wuq_pad, qn_pad)
    return rows_w, (wuk_pad, wuv, kn_pad), w_o.astype(BF16)


def mla_block(x, b, l, mw, ckv_past, kr_past, tile):
    rows_w, kv_w, w_o = mw
    n_past = 0 if ckv_past is None else ckv_past.shape[1]
    cos_t, sin_t = _rope_tables(n_past + jnp.arange(l))
    q, ckv, kr = mla_rows(x, *rows_w, jnp.tile(cos_t, (b, 1)), jnp.tile(sin_t, (b, 1)))
    kr_out = kr[:, MLA_NOPE:MLA_NOPE + MLA_ROPE]
    if ckv_past is None:
        k_all, v_all = mla_kv(ckv, kr, *kv_w)
        o = mla_attention(q.reshape(b, l, -1), k_all.reshape(b, l, -1), v_all.reshape(b, l, -1), True, tile, tile)
    else:
        lk = n_past + l
        ckv_all = jnp.concatenate([ckv_past, ckv.reshape(b, l, -1)], axis=1).reshape(b * lk, -1)
        kr_pad = jnp.pad(kr_past, ((0, 0), (0, 0), (MLA_NOPE, 128 - MLA_NOPE - MLA_ROPE)))
        kr_all = jnp.concatenate([kr_pad, kr.reshape(b, l, 128)], axis=1).reshape(b * lk, 128)
        k_all, v_all = mla_kv(ckv_all, kr_all, *kv_w, tm=lk // 5)
        o = mla_attention(q.reshape(b, l, -1), k_all.reshape(b, lk, -1), v_all.reshape(b, lk, -1), False, l, lk)
    x = rms_matmul(o.reshape(b * l, -1), w_o, res=x, tm=1024, name="mla_out_proj")
    return x, ckv, kr_out


def kernel(x_prompt, x_sample, mem_prompt, state_conv_a, state_conv_ssm, state_ssm, cache_mla_ckv, cache_mla_krope, cache_mem_k, cache_mem_v, norm_mix, norm_mem, norm_ffn, w_in_e, conv_a_w, conv_s_w, conv_s_b, dt_bias, a_log, d_skip, ssm_norm, w_out_e, w_down_o, q_lora_norm, kv_lora_norm, w_uq, w_ukv, q_nope_norm, q_rope_norm, k_nope_norm, k_rope_norm, w_o_mla, mem_src_norm, w_mem_q, w_mem_k, w_mem_v, mem_q_norm, mem_k_norm, w_mem_o, w_ffn1, w_ffn3, w_ffn2, w_router, w_exp1, w_exp3, w_exp2):
    bp, lp, D = x_prompt.shape
    bs, ls, _ = x_sample.shape
    past = cache_mla_ckv.shape[2]
    bf = lambda w: w.astype(BF16)
    xp = x_prompt.reshape(bp * lp, D)
    xs = x_sample.reshape(bs * ls, D)

    ew = even_weights(w_in_e[0], w_out_e[0], norm_mix[0], conv_a_w[0], conv_s_w[0], conv_s_b[0], dt_bias[0],
                      a_log[0], d_skip[0], ssm_norm[0])
    xp, p_ca, p_cs, p_st = even_mixer_block(
        xp, bp, lp, ew, jnp.zeros((bp, 2, A_WIDTH), F32), jnp.zeros((bp, 3, SSM_XBC), F32),
        jnp.zeros((bp, SSM_HEADS, SSM_HEAD_DIM, SSM_STATE), F32), 1024)
    xs, s_ca, s_cs, s_st = even_mixer_block(xs, bs, ls, ew, state_conv_a[0], state_conv_ssm[0], state_ssm[0], 512)

    p_mk, p_mv = [], []

    def memory_block(i, xp, xs):
        mk, mv, mk_b, mv_b = mem_kv(mem_prompt, mem_src_norm[i], bf(w_mem_k[i]), bf(w_mem_v[i]), mem_k_norm[i])
        p_mk.append(mk.reshape(bp, MEM_TOKENS, MEM_HEADS, MEM_HEAD_DIM))
        p_mv.append(mv.reshape(bp, MEM_TOKENS, MEM_HEADS, MEM_HEAD_DIM))
        wq, wo = bf(w_mem_q[i]), bf(w_mem_o[i])
        xp = mem_attend(xp, norm_mem[i], mk_b, mv_b, wq, mem_q_norm[i], wo, lp)
        ck = bf(cache_mem_k[i]).reshape(bs, MEM_TOKENS, D)
        cv = bf(cache_mem_v[i]).reshape(bs, MEM_TOKENS, D)
        xs = mem_attend(xs, norm_mem[i], ck, cv, wq, mem_q_norm[i], wo, ls)
        return xp, xs

    xp, xs = memory_block(0, xp, xs)
    w1, w3, w2 = bf(w_ffn1[0]), bf(w_ffn3[0]), bf(w_ffn2[0])
    xp = ffn_dense(xp, norm_ffn[0], w1, w3, w2)
    xs = ffn_dense(xs, norm_ffn[0], w1, w3, w2)

    mw = mla_weights(norm_mix[1], w_down_o[0], q_lora_norm[0], kv_lora_norm[0], w_uq[0], w_ukv[0], q_nope_norm[0],
                     q_rope_norm[0], k_nope_norm[0], k_rope_norm[0], w_o_mla[0])
    xp, p_ckv, p_kr = mla_block(xp, bp, lp, mw, None, None, min(lp, 1024))
    xs, s_ckv, s_kr = mla_block(xs, bs, ls, mw, cache_mla_ckv[0], cache_mla_krope[0], None)

    xp, xs = memory_block(1, xp, xs)
    we1, we3, we2 = bf(w_exp1[0]), bf(w_exp3[0]), bf(w_exp2[0])
    xp = moe_ffn(xp, norm_ffn[1], w_router[0], we1, we3, we2, 512)
    xs = moe_ffn(xs, norm_ffn[1], w_router[0], we1, we3, we2, 256)

    return (xp.reshape(bp, lp, D), xs.reshape(bs, ls, D),
            p_ca[None], p_cs[None], p_st[None],
            p_ckv.reshape(1, bp, lp, -1), p_kr.reshape(1, bp, lp, -1), jnp.stack(p_mk), jnp.stack(p_mv),
            s_ca[None], s_cs[None], s_st[None],
            s_ckv.reshape(1, bs, ls, -1), s_kr.reshape(1, bs, ls, -1))
```

```python
import functools
import math

import jax
import jax.numpy as jnp
from jax import lax
from jax.experimental import pallas as pl
from jax.experimental.pallas import tpu as pltpu

F32 = jnp.float32
BF16 = jnp.bfloat16
EPS = 1e-6

D_MODEL = 1024
CHUNK = 64
A_WIDTH = 1024
SSM_HEADS = 32
SSM_HEAD_DIM = 64
SSM_INNER = 2048
SSM_STATE = 128
SSM_GROUPS = 4
SSM_XBC = 3072
HEAD_LANES = 128
PROJ_W = 3 * A_WIDTH + SSM_INNER + SSM_XBC + HEAD_LANES
COL_Z = 3 * A_WIDTH
COL_XBC = COL_Z + SSM_INNER
COL_DT = COL_XBC + SSM_XBC
MLA_HEADS = 16
MLA_Q_LORA = 512
MLA_KV_LORA = 256
MLA_NOPE = 64
MLA_ROPE = 32
MLA_V = 64
MLA_SCALE = (MLA_NOPE + MLA_ROPE) ** -0.5
ROPE_THETA = 10000.0
MEM_TOKENS = 256
MEM_HEADS = 4
MEM_HEAD_DIM = 256
N_EXPERTS = 8
VMEM_LIMIT = 56 * 1024 * 1024
NT_DIMS = (((1,), (1,)), ((), ()))


def _params(*sem):
    return pltpu.CompilerParams(dimension_semantics=sem, vmem_limit_bytes=VMEM_LIMIT)


def _rms(x, g):
    return x * lax.rsqrt(jnp.mean(x * x, axis=-1, keepdims=True) + EPS) * g


def _split3(x):
    hi = x.astype(BF16)
    r1 = x - hi.astype(F32)
    mid = r1.astype(BF16)
    lo = (r1 - mid.astype(F32)).astype(BF16)
    return hi, mid, lo


def _dot_sel(x, sel_bf16):
    hi, mid, lo = _split3(x)
    d = lambda a: jnp.dot(a, sel_bf16, preferred_element_type=F32)
    return d(hi) + d(mid) + d(lo)


def _sel_dot_nt(sel_bf16, x):
    hi, mid, lo = _split3(x)
    d = lambda a: lax.dot_general(sel_bf16, a, NT_DIMS, preferred_element_type=F32)
    return d(hi) + d(mid) + d(lo)


def _sel_dot(sel_bf16, x):
    hi, mid, lo = _split3(x)
    d = lambda a: jnp.dot(sel_bf16, a, preferred_element_type=F32)
    return d(hi) + d(mid) + d(lo)


def _rms_matmul_body(*refs, use_norm, use_res):
    it = iter(refs)
    x_ref = next(it)
    g_ref = next(it) if use_norm else None
    w_ref = next(it)
    r_ref = next(it) if use_res else None
    o_ref = next(it)
    xn_ref = next(it) if use_norm else None
    if use_norm:
        @pl.when(pl.program_id(1) == 0)
        def _():
            xn_ref[...] = _rms(x_ref[...], g_ref[...]).astype(BF16)
        xb = xn_ref[...]
    else:
        xb = x_ref[...]
    acc = jnp.dot(xb, w_ref[...], preferred_element_type=F32)
    if use_res:
        acc = acc + r_ref[...]
    o_ref[...] = acc.astype(o_ref.dtype)


def rms_matmul(x, w, g=None, res=None, out_dtype=F32, tm=512, tn=None, name="rms_matmul"):
    M, K = x.shape
    N = w.shape[1]
    tn = tn or N
    tm = min(tm, M)
    in_specs = [pl.BlockSpec((tm, K), lambda i, j: (i, 0))]
    args = [x]
    if g is not None:
        in_specs.append(pl.BlockSpec((1, K), lambda i, j: (0, 0)))
        args.append(g.reshape(1, K))
    in_specs.append(pl.BlockSpec((K, tn), lambda i, j: (0, j)))
    args.append(w)
    if res is not None:
        in_specs.append(pl.BlockSpec((tm, tn), lambda i, j: (i, j)))
        args.append(res)
    scratch = [pltpu.VMEM((tm, K), BF16)] if g is not None else []
    return pl.pallas_call(
        functools.partial(_rms_matmul_body, use_norm=g is not None, use_res=res is not None),
        name=name,
        grid=(M // tm, N // tn),
        in_specs=in_specs,
        out_specs=pl.BlockSpec((tm, tn), lambda i, j: (i, j)),
        out_shape=jax.ShapeDtypeStruct((M, N), out_dtype),
        scratch_shapes=scratch,
        compiler_params=_params("parallel", "arbitrary"),
    )(*args)


def _softplus(v):
    return jnp.maximum(v, 0.0) + jnp.log1p(jnp.exp(-jnp.abs(v)))


def _mixer_body(proj_ref, wa_ref, ws_ref, bs_ref, dtb_ref, alog_ref, dskip_ref, gn_ref,
                tri_ref, selrow_ref, expand_ref, eye_ref,
                ca_prev_ref, cs_prev_ref, st_prev_ref,
                mix_ref, ca_out_ref, cs_out_ref, st_ref,
                abuf, sbuf):
    c = pl.program_id(1)
    Q = CHUNK

    @pl.when(c == 0)
    def _():
        abuf[0:8, :] = ca_prev_ref[0]
        sbuf[0:8, :] = cs_prev_ref[0]
        st_ref[0] = st_prev_ref[0]

    g_b = proj_ref[0, :, 0:A_WIDTH]
    abuf[8:8 + Q, :] = proj_ref[0, :, A_WIDTH:2 * A_WIDTH] * proj_ref[0, :, 2 * A_WIDTH:3 * A_WIDTH]
    conv_a = wa_ref[0:1, :] * abuf[6:6 + Q, :]
    conv_a = conv_a + wa_ref[1:2, :] * abuf[7:7 + Q, :]
    conv_a = conv_a + wa_ref[2:3, :] * abuf[8:8 + Q, :]
    mix_ref[0, :, 0:A_WIDTH] = (g_b * conv_a).astype(BF16)
    tail_a = abuf[Q:Q + 8, :]
    ca_out_ref[0] = tail_a
    abuf[0:8, :] = tail_a

    sbuf[8:8 + Q, :] = proj_ref[0, :, COL_XBC:COL_DT]
    xc = ws_ref[0:1, :] * sbuf[5:5 + Q, :]
    xc = xc + ws_ref[1:2, :] * sbuf[6:6 + Q, :]
    xc = xc + ws_ref[2:3, :] * sbuf[7:7 + Q, :]
    xc = xc + ws_ref[3:4, :] * sbuf[8:8 + Q, :]
    xc = xc + bs_ref[...]
    xc = xc * jax.nn.sigmoid(xc)
    tail_s = sbuf[Q:Q + 8, :]
    cs_out_ref[0] = tail_s
    sbuf[0:8, :] = tail_s

    dt = _softplus(proj_ref[0, :, COL_DT:COL_DT + HEAD_LANES] + dtb_ref[...])
    la = dt * (-jnp.exp(alog_ref[...]))
    cs = _sel_dot(tri_ref[...], la)
    cs_last = cs[Q - 1:Q, :]
    e_in = jnp.exp(cs)
    w_st = dt * jnp.exp(cs_last - cs)
    dec = jnp.broadcast_to(jnp.exp(cs_last), (8, HEAD_LANES))
    ex = _dot_sel(jnp.concatenate([cs, e_in, w_st, dec], axis=0), expand_ref[...])
    cs_x, e_in_x, w_st_x, dec_x = ex[0:Q], ex[Q:2 * Q], ex[2 * Q:3 * Q], ex[3 * Q:3 * Q + 1]
    cs_r = _sel_dot_nt(selrow_ref[...], cs)
    dt_r = _sel_dot_nt(selrow_ref[...], dt)
    cs_rp = jnp.concatenate([cs_r[0:16], cs_r[16:32]], axis=1)
    dt_rp = jnp.concatenate([dt_r[0:16], dt_r[16:32]], axis=1)

    lane = lax.broadcasted_iota(jnp.int32, (Q, 128), 1)
    row = lax.broadcasted_iota(jnp.int32, (Q, 128), 0)
    tril2 = jnp.where(lane >= Q, lane - Q, lane) <= row
    left = lane < Q
    eye = eye_ref[...]

    for g in range(SSM_GROUPS):
        gl = slice(SSM_INNER + g * SSM_STATE, SSM_INNER + (g + 1) * SSM_STATE)
        Bg = xc[:, gl].astype(BF16)
        Cg = xc[:, gl.start + SSM_GROUPS * SSM_STATE: gl.stop + SSM_GROUPS * SSM_STATE].astype(BF16)
        sc = lax.dot_general(Cg, Bg, NT_DIMS, preferred_element_type=F32)
        sc2 = jnp.concatenate([sc, sc], axis=1)
        BgT = lax.dot_general(eye, Bg, NT_DIMS, preferred_element_type=F32).astype(BF16)
        hs = slice(g * 512, (g + 1) * 512)
        st_g = st_ref[0, :, hs]
        y_off = jnp.dot(Cg, st_g.astype(BF16), preferred_element_type=F32)
        xg = xc[:, hs]
        xw = (xg * w_st_x[:, hs]).astype(BF16)
        st_ref[0, :, hs] = dec_x[:, hs] * st_g + jnp.dot(BgT, xw, preferred_element_type=F32)
        ys = []
        for kk in range(4):
            k = 4 * g + kk
            pl_ = slice(k * 128, (k + 1) * 128)
            diff = cs_x[:, pl_] - cs_rp[k:k + 1, :]
            decay = jnp.exp(jnp.where(tril2, diff, -jnp.inf))
            m_pair = (sc2 * decay * dt_rp[k:k + 1, :]).astype(BF16)
            xp = xc[:, pl_]
            rhs = jnp.concatenate([jnp.where(left, xp, 0.0), jnp.where(left, 0.0, xp)], axis=0).astype(BF16)
            y_diag = jnp.dot(m_pair, rhs, preferred_element_type=F32)
            y = y_diag + y_off[:, kk * 128:(kk + 1) * 128] * e_in_x[:, pl_]
            y = y + dskip_ref[:, pl_] * xp
            z = proj_ref[0, :, COL_Z + k * 128: COL_Z + (k + 1) * 128]
            ys.append(y * (z * jax.nn.sigmoid(z)))
        yg = jnp.concatenate(ys, axis=1)
        yn = yg * lax.rsqrt(jnp.mean(yg * yg, axis=-1, keepdims=True) + EPS) * gn_ref[:, hs]
        mix_ref[0, :, A_WIDTH + g * 512: A_WIDTH + (g + 1) * 512] = yn.astype(BF16)


def mixer_even(proj, conv_a_w, conv_s_w, conv_s_b, dt_bias, a_log, d_skip, ssm_norm,
               conv_a_prev8, conv_s_prev8, ssm_prev_t):
    B, L, _ = proj.shape
    nc = L // CHUNK
    pad_h = lambda v: jnp.pad(v.reshape(1, SSM_HEADS), ((0, 0), (0, HEAD_LANES - SSM_HEADS)))
    tri = (jnp.arange(CHUNK)[:, None] >= jnp.arange(CHUNK)[None, :]).astype(BF16)
    heads = jnp.arange(HEAD_LANES)
    order = jnp.concatenate([jnp.arange(0, SSM_HEADS, 2), jnp.arange(1, SSM_HEADS, 2)])
    selrow = (order[:, None] == heads[None, :]).astype(BF16)
    expand = (heads[:, None] == (jnp.arange(SSM_INNER) // SSM_HEAD_DIM)[None, :]).astype(BF16)
    eye = jnp.eye(SSM_STATE, dtype=BF16)
    dskip_x = jnp.repeat(d_skip.astype(F32), SSM_HEAD_DIM).reshape(1, SSM_INNER)
    const = lambda shape: pl.BlockSpec(shape, lambda b, c: (0,) * len(shape))
    per_b = lambda shape: pl.BlockSpec((1,) + shape, lambda b, c: (b,) + (0,) * len(shape))
    return pl.pallas_call(
        _mixer_body,
        name="ssd_mixer",
        grid=(B, nc),
        in_specs=[
            pl.BlockSpec((1, CHUNK, PROJ_W), lambda b, c: (b, c, 0)),
            const((3, A_WIDTH)), const((4, SSM_XBC)), const((1, SSM_XBC)),
            const((1, HEAD_LANES)), const((1, HEAD_LANES)), const((1, SSM_INNER)), const((1, SSM_INNER)),
            const((CHUNK, CHUNK)), const((SSM_HEADS, HEAD_LANES)), const((HEAD_LANES, SSM_INNER)),
            const((SSM_STATE, SSM_STATE)),
            per_b((8, A_WIDTH)), per_b((8, SSM_XBC)), per_b((SSM_STATE, SSM_INNER)),
        ],
        out_specs=[
            pl.BlockSpec((1, CHUNK, A_WIDTH + SSM_INNER), lambda b, c: (b, c, 0)),
            per_b((8, A_WIDTH)), per_b((8, SSM_XBC)), per_b((SSM_STATE, SSM_INNER)),
        ],
        out_shape=[
            jax.ShapeDtypeStruct((B, L, A_WIDTH + SSM_INNER), BF16),
            jax.ShapeDtypeStruct((B, 8, A_WIDTH), F32),
            jax.ShapeDtypeStruct((B, 8, SSM_XBC), F32),
            jax.ShapeDtypeStruct((B, SSM_STATE, SSM_INNER), F32),
        ],
        scratch_shapes=[
            pltpu.VMEM((CHUNK + 8, A_WIDTH), F32),
            pltpu.VMEM((CHUNK + 8, SSM_XBC), F32),
        ],
        compiler_params=_params("parallel", "arbitrary"),
    )(proj, conv_a_w, conv_s_w, conv_s_b.reshape(1, SSM_XBC), pad_h(dt_bias), pad_h(a_log), dskip_x,
      ssm_norm.reshape(1, SSM_INNER), tri, selrow, expand, eye, conv_a_prev8, conv_s_prev8, ssm_prev_t)


def even_weights(w_in, w_out, norm_g, conv_a_w, conv_s_w, conv_s_b, dt_bias, a_log, d_skip, ssm_norm):
    w_in_pad = jnp.pad(w_in, ((0, 0), (0, PROJ_W - w_in.shape[1]))).astype(BF16)
    return (norm_g, w_in_pad, w_out.astype(BF16), (conv_a_w, conv_s_w, conv_s_b, dt_bias, a_log, d_skip, ssm_norm))


def even_mixer_block(x, b, l, ew, ca_prev, cs_prev, st_prev, tm):
    norm_g, w_in_pad, w_out, mixer_w = ew
    proj = rms_matmul(x, w_in_pad, g=norm_g, tm=tm, tn=PROJ_W // 5, name="mixer_in_proj")
    ca8 = jnp.pad(ca_prev, ((0, 0), (6, 0), (0, 0)))
    cs8 = jnp.pad(cs_prev, ((0, 0), (5, 0), (0, 0)))
    st_t = jnp.transpose(st_prev, (0, 3, 1, 2)).reshape(b, SSM_STATE, SSM_INNER)
    mix, ca, cs, st = mixer_even(proj.reshape(b, l, PROJ_W), *mixer_w, ca8, cs8, st_t)
    x = rms_matmul(mix.reshape(b * l, -1), w_out, res=x, tm=tm, name="mixer_out_proj")
    st = jnp.transpose(st.reshape(b, SSM_STATE, SSM_HEADS, SSM_HEAD_DIM), (0, 2, 3, 1))
    return x, ca[:, 6:8], cs[:, 5:8], st


def _mem_kv_body(mem_ref, g_ref, wk_ref, wv_ref, kn_ref, k_ref, v_ref, kb_ref, vb_ref):
    mm = _rms(mem_ref[0], g_ref[...]).astype(BF16)
    k = jnp.dot(mm, wk_ref[...], preferred_element_type=F32)
    v = jnp.dot(mm, wv_ref[...], preferred_element_type=F32)
    for h in range(MEM_HEADS):
        sl = slice(h * MEM_HEAD_DIM, (h + 1) * MEM_HEAD_DIM)
        kh = _rms(k[:, sl], kn_ref[...])
        k_ref[0, :, sl] = kh
        kb_ref[0, :, sl] = kh.astype(BF16)
    v_ref[0] = v
    vb_ref[0] = v.astype(BF16)


def mem_kv(mem, src_norm, w_k, w_v, k_norm):
    B, Mt, D = mem.shape
    const = lambda shape: pl.BlockSpec(shape, lambda b: (0,) * len(shape))
    blk = pl.BlockSpec((1, Mt, D), lambda b: (b, 0, 0))
    return pl.pallas_call(
        _mem_kv_body,
        name="mem_kv",
        grid=(B,),
        in_specs=[blk, const((1, D)), const((D, D)), const((D, D)), const((1, MEM_HEAD_DIM))],
        out_specs=[blk, blk, blk, blk],
        out_shape=[jax.ShapeDtypeStruct((B, Mt, D), F32), jax.ShapeDtypeStruct((B, Mt, D), F32),
                   jax.ShapeDtypeStruct((B, Mt, D), BF16), jax.ShapeDtypeStruct((B, Mt, D), BF16)],
        compiler_params=_params("parallel"),
    )(mem, src_norm.reshape(1, D), w_k, w_v, k_norm.reshape(1, MEM_HEAD_DIM))


def _mem_attn_body(x_ref, g_ref, wq_ref, qn_ref, k_ref, v_ref, wo_ref, o_ref, *, nb):
    x = x_ref[...]
    tm = x.shape[0]
    rows = tm // nb
    hn = _rms(x, g_ref[...]).astype(BF16)
    q = jnp.dot(hn, wq_ref[...], preferred_element_type=F32)
    scale = MEM_HEAD_DIM ** -0.5
    outs = []
    for h in range(MEM_HEADS):
        sl = slice(h * MEM_HEAD_DIM, (h + 1) * MEM_HEAD_DIM)
        qh = _rms(q[:, sl], qn_ref[...]).astype(BF16)
        parts = []
        for s in range(nb):
            qs = qh[s * rows:(s + 1) * rows]
            sc = lax.dot_general(qs, k_ref[s, :, sl], NT_DIMS, preferred_element_type=F32) * scale
            sc = sc - jnp.max(sc, axis=-1, keepdims=True)
            p = jnp.exp(sc)
            p = p / jnp.sum(p, axis=-1, keepdims=True)
            parts.append(jnp.dot(p.astype(BF16), v_ref[s, :, sl], preferred_element_type=F32))
        outs.append(parts[0] if nb == 1 else jnp.concatenate(parts, axis=0))
    o = jnp.concatenate(outs, axis=1).astype(BF16)
    o_ref[...] = x + jnp.dot(o, wo_ref[...], preferred_element_type=F32)


def mem_attend(x, norm_g, k_b, v_b, w_q, q_norm, w_o, rows_per_seq, tm=512):
    M, D = x.shape
    tm = min(tm, M)
    nb = max(1, tm // rows_per_seq)
    tiles_per_seq = max(1, rows_per_seq // tm)
    const = lambda shape: pl.BlockSpec(shape, lambda i: (0,) * len(shape))
    kv_spec = pl.BlockSpec((nb, MEM_TOKENS, D), lambda i: (i // tiles_per_seq, 0, 0))
    return pl.pallas_call(
        functools.partial(_mem_attn_body, nb=nb),
        name="mem_attn",
        grid=(M // tm,),
        in_specs=[pl.BlockSpec((tm, D), lambda i: (i, 0)), const((1, D)), const((D, D)),
                  const((1, MEM_HEAD_DIM)), kv_spec, kv_spec, const((D, D))],
        out_specs=pl.BlockSpec((tm, D), lambda i: (i, 0)),
        out_shape=jax.ShapeDtypeStruct((M, D), F32),
        compiler_params=_params("parallel"),
    )(x, norm_g.reshape(1, D), w_q, q_norm.reshape(1, MEM_HEAD_DIM), k_b, v_b, w_o)


def _swiglu_step(xn, w1, w3, w2):
    h1 = jnp.dot(xn, w1, preferred_element_type=F32)
    h3 = jnp.dot(xn, w3, preferred_element_type=F32)
    a = (h1 * jax.nn.sigmoid(h1) * h3).astype(BF16)
    return jnp.dot(a, w2, preferred_element_type=F32)


def _ffn_dense_body(x_ref, g_ref, w1_ref, w3_ref, w2_ref, o_ref, xn_ref, acc_ref):
    f = pl.program_id(1)

    @pl.when(f == 0)
    def _():
        xn_ref[...] = _rms(x_ref[...], g_ref[...]).astype(BF16)
        acc_ref[...] = jnp.zeros_like(acc_ref)

    acc_ref[...] += _swiglu_step(xn_ref[...], w1_ref[...], w3_ref[...], w2_ref[...])

    @pl.when(f == pl.num_programs(1) - 1)
    def _():
        o_ref[...] = x_ref[...] + acc_ref[...]


def ffn_dense(x, norm_g, w1, w3, w2, tm=512, tf=1408):
    M, D = x.shape
    F = w1.shape[1]
    tm = min(tm, M)
    return pl.pallas_call(
        _ffn_dense_body,
        name="ffn_dense",
        grid=(M // tm, F // tf),
        in_specs=[pl.BlockSpec((tm, D), lambda i, f: (i, 0)), pl.BlockSpec((1, D), lambda i, f: (0, 0)),
                  pl.BlockSpec((D, tf), lambda i, f: (0, f)), pl.BlockSpec((D, tf), lambda i, f: (0, f)),
                  pl.BlockSpec((tf, D), lambda i, f: (f, 0))],
        out_specs=pl.BlockSpec((tm, D), lambda i, f: (i, 0)),
        out_shape=jax.ShapeDtypeStruct((M, D), F32),
        scratch_shapes=[pltpu.VMEM((tm, D), BF16), pltpu.VMEM((tm, D), F32)],
        compiler_params=_params("parallel", "arbitrary"),
    )(x, norm_g.reshape(1, D), w1, w3, w2)


def _ffn_expert_body(te_ref, tv_ref, x_ref, w1_ref, w3_ref, w2_ref, o_ref, xn_ref, acc_ref):
    i = pl.program_id(0)
    f = pl.program_id(1)
    last = f == pl.num_programs(1) - 1

    @pl.when(tv_ref[i] > 0)
    def _():
        @pl.when(f == 0)
        def _():
            xn_ref[...] = x_ref[...].astype(BF16)
            acc_ref[...] = jnp.zeros_like(acc_ref)

        acc_ref[...] += _swiglu_step(xn_ref[...], w1_ref[0], w3_ref[0], w2_ref[0])

        @pl.when(last)
        def _():
            o_ref[...] = acc_ref[...]

    @pl.when(jnp.logical_and(tv_ref[i] == 0, last))
    def _():
        o_ref[...] = jnp.zeros_like(o_ref)


def ffn_experts(xs, tile_expert, tile_valid, w1, w3, w2, tm, tf=1792):
    N, D = xs.shape
    F = w1.shape[2]
    grid_spec = pltpu.PrefetchScalarGridSpec(
        num_scalar_prefetch=2,
        grid=(N // tm, F // tf),
        in_specs=[pl.BlockSpec((tm, D), lambda i, f, te, tv: (i, 0)),
                  pl.BlockSpec((1, D, tf), lambda i, f, te, tv: (te[i], 0, f)),
                  pl.BlockSpec((1, D, tf), lambda i, f, te, tv: (te[i], 0, f)),
                  pl.BlockSpec((1, tf, D), lambda i, f, te, tv: (te[i], f, 0))],
        out_specs=pl.BlockSpec((tm, D), lambda i, f, te, tv: (i, 0)),
        scratch_shapes=[pltpu.VMEM((tm, D), BF16), pltpu.VMEM((tm, D), F32)],
    )
    return pl.pallas_call(
        _ffn_expert_body,
        name="ffn_experts",
        grid_spec=grid_spec,
        out_shape=jax.ShapeDtypeStruct((N, D), F32),
        compiler_params=_params("arbitrary", "arbitrary"),
    )(tile_expert, tile_valid, xs, w1, w3, w2)


def _router_body(x_ref, g_ref, wr_ref, hn_ref, route_ref):
    hn = _rms(x_ref[...], g_ref[...])
    hn_ref[...] = hn
    xh, xm, xl = _split3(hn)
    wh, wm, wl = wr_ref[0], wr_ref[1], wr_ref[2]
    d = lambda a, b: jnp.dot(a, b, preferred_element_type=F32)
    logits = d(xh, wh) + (d(xh, wm) + d(xm, wh)) + (d(xm, wm) + d(xh, wl) + d(xl, wh))
    lane = lax.broadcasted_iota(jnp.int32, logits.shape, 1)
    lane_f = lane.astype(F32)
    neg = -jnp.inf
    lg = jnp.where(lane < N_EXPERTS, logits, neg)
    m1 = jnp.max(lg, axis=-1, keepdims=True)
    i1 = jnp.min(jnp.where(lg == m1, lane_f, 128.0), axis=-1, keepdims=True)
    lg2 = jnp.where(lane_f == i1, neg, lg)
    m2 = jnp.max(lg2, axis=-1, keepdims=True)
    i2 = jnp.min(jnp.where(lg2 == m2, lane_f, 128.0), axis=-1, keepdims=True)
    e = jnp.exp(m2 - m1)
    den = 1.0 + e
    g1 = 1.0 / den
    g2 = e / den
    route = jnp.where(lane == 0, g1, jnp.where(lane == 1, g2, jnp.where(lane == 2, i1, jnp.where(lane == 3, i2, 0.0))))
    route_ref[...] = route


def router(x, norm_g, w_router, tm=512):
    M, D = x.shape
    tm = min(tm, M)
    wr = jnp.pad(w_router.astype(F32), ((0, 0), (0, 128 - N_EXPERTS)))
    wh = wr.astype(BF16)
    r1 = wr - wh.astype(F32)
    wm = r1.astype(BF16)
    wl = (r1 - wm.astype(F32)).astype(BF16)
    w3 = jnp.stack([wh, wm, wl])
    return pl.pallas_call(
        _router_body,
        name="router",
        grid=(M // tm,),
        in_specs=[pl.BlockSpec((tm, D), lambda i: (i, 0)), pl.BlockSpec((1, D), lambda i: (0, 0)),
                  pl.BlockSpec((3, D, 128), lambda i: (0, 0, 0))],
        out_specs=[pl.BlockSpec((tm, D), lambda i: (i, 0)), pl.BlockSpec((tm, 128), lambda i: (i, 0))],
        out_shape=[jax.ShapeDtypeStruct((M, D), F32), jax.ShapeDtypeStruct((M, 128), F32)],
        compiler_params=_params("parallel"),
    )(x, norm_g.reshape(1, D), w3)


def _row_put(src_ref, r, dst_hbm, idx, sem):
    return pltpu.make_async_copy(src_ref.at[pl.ds(r, 1), :], dst_hbm.at[pl.ds(idx, 1), :], sem)


def _row_get(src_hbm, idx, dst_ref, r, sem):
    return pltpu.make_async_copy(src_hbm.at[pl.ds(idx, 1), :], dst_ref.at[pl.ds(r, 1), :], sem)


def _dispatch_body(i1_ref, i2_ref, hn_ref, xs_in_hbm, xs_hbm, sem):
    del xs_in_hbm
    tm = hn_ref.shape[0]

    def start(r, _):
        _row_put(hn_ref, r, xs_hbm, i1_ref[0, 0, r], sem).start()
        _row_put(hn_ref, r, xs_hbm, i2_ref[0, 0, r], sem).start()
        return 0

    lax.fori_loop(0, tm, start, 0, unroll=8)

    def wait(r, _):
        _row_put(hn_ref, r, xs_hbm, 0, sem).wait()
        _row_put(hn_ref, r, xs_hbm, 0, sem).wait()
        return 0

    lax.fori_loop(0, tm, wait, 0, unroll=8)


def dispatch_rows(hn, pos1, pos2, xs, tm=512):
    M, D = hn.shape
    tm = min(tm, M)
    idx_spec = pl.BlockSpec((1, 1, tm), lambda i: (i, 0, 0), memory_space=pltpu.SMEM)
    return pl.pallas_call(
        _dispatch_body,
        name="dispatch_rows",
        grid=(M // tm,),
        in_specs=[idx_spec, idx_spec, pl.BlockSpec((tm, D), lambda i: (i, 0)), pl.BlockSpec(memory_space=pl.ANY)],
        out_specs=pl.BlockSpec(memory_space=pl.ANY),
        out_shape=jax.ShapeDtypeStruct(xs.shape, xs.dtype),
        input_output_aliases={3: 0},
        scratch_shapes=[pltpu.SemaphoreType.DMA(())],
        compiler_params=_params("arbitrary"),
    )(pos1.reshape(M // tm, 1, tm), pos2.reshape(M // tm, 1, tm), hn, xs)


def _combine_body(i1_ref, i2_ref, x_ref, route_ref, ys_hbm, o_ref, b1, b2, sem1, sem2):
    tm = o_ref.shape[0]

    def start(r, _):
        _row_get(ys_hbm, i1_ref[0, 0, r], b1, r, sem1).start()
        _row_get(ys_hbm, i2_ref[0, 0, r], b2, r, sem2).start()
        return 0

    lax.fori_loop(0, tm, start, 0, unroll=8)

    def wait(r, _):
        _row_get(ys_hbm, 0, b1, r, sem1).wait()
        _row_get(ys_hbm, 0, b2, r, sem2).wait()
        return 0

    lax.fori_loop(0, tm, wait, 0, unroll=8)
    o_ref[...] = x_ref[...] + route_ref[:, 0:1] * b1[...] + route_ref[:, 1:2] * b2[...]


def combine_rows(x, route, ys, pos1, pos2, tm=256):
    M, D = x.shape
    tm = min(tm, M)
    idx_spec = pl.BlockSpec((1, 1, tm), lambda i: (i, 0, 0), memory_space=pltpu.SMEM)
    return pl.pallas_call(
        _combine_body,
        name="combine_rows",
        grid=(M // tm,),
        in_specs=[idx_spec, idx_spec, pl.BlockSpec((tm, D), lambda i: (i, 0)),
                  pl.BlockSpec((tm, 128), lambda i: (i, 0)), pl.BlockSpec(memory_space=pl.ANY)],
        out_specs=pl.BlockSpec((tm, D), lambda i: (i, 0)),
        out_shape=jax.ShapeDtypeStruct((M, D), F32),
        scratch_shapes=[pltpu.VMEM((tm, D), F32), pltpu.VMEM((tm, D), F32),
                        pltpu.SemaphoreType.DMA(()), pltpu.SemaphoreType.DMA(())],
        compiler_params=_params("arbitrary"),
    )(pos1.reshape(M // tm, 1, tm), pos2.reshape(M // tm, 1, tm), x, route, ys)


def moe_ffn(xs_list, norm_g, w_router, w1, w3, w2, tm_e):
    D = xs_list[0].shape[1]
    routed = [router(x, norm_g, w_router) for x in xs_list]
    top_i = jnp.concatenate([route[:, 2:4] for _, route in routed], axis=0).astype(jnp.int32)
    T = top_i.shape[0]
    e_flat = top_i.reshape(-1)
    onehot = (e_flat[:, None] == jnp.arange(N_EXPERTS, dtype=jnp.int32)[None, :]).astype(jnp.int32)
    csum = jnp.cumsum(onehot, axis=0)
    counts = csum[-1]
    rank = jnp.take_along_axis(csum, e_flat[:, None], axis=1)[:, 0] - 1
    ptiles = (counts + tm_e - 1) // tm_e
    tile_end = jnp.cumsum(ptiles)
    pos = (tile_end - ptiles)[e_flat] * tm_e + rank
    pos1, pos2 = pos[0::2], pos[1::2]
    n_tiles = (2 * T) // tm_e + N_EXPERTS
    tile_ids = jnp.arange(n_tiles, dtype=jnp.int32)
    tile_valid = (tile_ids < tile_end[-1]).astype(jnp.int32)
    last_e = jnp.max(jnp.where(counts > 0, jnp.arange(N_EXPERTS, dtype=jnp.int32), 0))
    tile_expert = jnp.sum((tile_end[None, :] <= tile_ids[:, None]).astype(jnp.int32), axis=1)
    tile_expert = jnp.minimum(tile_expert, last_e)
    xs = jnp.zeros((n_tiles * tm_e, D), F32)
    t0 = 0
    for hn, _ in routed:
        t1 = t0 + hn.shape[0]
        xs = dispatch_rows(hn, pos1[t0:t1], pos2[t0:t1], xs)
        t0 = t1
    ys = ffn_experts(xs, tile_expert, tile_valid, w1, w3, w2, tm_e)
    outs = []
    t0 = 0
    for x, (_, route) in zip(xs_list, routed):
        t1 = t0 + x.shape[0]
        outs.append(combine_rows(x, route, ys, pos1[t0:t1], pos2[t0:t1]))
        t0 = t1
    return outs


def _rope_lanes(v, cos, sin, lane):
    partner = jnp.where(lane < 80, pltpu.roll(v, 112, 1), pltpu.roll(v, 16, 1))
    return v * cos + partner * sin


def _mla_rows_body(x_ref, g_ref, wd_ref, qln_ref, kvln_ref, krn_ref, wuq_ref, qn_ref, cos_ref, sin_ref,
                   q_ref, ckv_ref, kr_ref):
    hn = _rms(x_ref[...], g_ref[...]).astype(BF16)
    d = jnp.dot(hn, wd_ref[...], preferred_element_type=F32)
    cq = _rms(d[:, 0:MLA_Q_LORA], qln_ref[...]).astype(BF16)
    ckv_ref[...] = _rms(d[:, MLA_Q_LORA:MLA_Q_LORA + MLA_KV_LORA], kvln_ref[...])
    cos = cos_ref[...]
    sin = sin_ref[...]
    lane = lax.broadcasted_iota(jnp.int32, cos.shape, 1)
    kr = d[:, MLA_Q_LORA + MLA_KV_LORA:]
    kr = kr * lax.rsqrt(jnp.sum(kr * kr, axis=-1, keepdims=True) * (1.0 / MLA_ROPE) + EPS) * krn_ref[...]
    kr_ref[...] = _rope_lanes(kr, cos, sin, lane)
    q = jnp.dot(cq, wuq_ref[...], preferred_element_type=F32)
    nope = lane < MLA_NOPE
    for h in range(MLA_HEADS):
        qh = q[:, h * 128:(h + 1) * 128]
        sq = qh * qh
        ss_n = jnp.sum(jnp.where(nope, sq, 0.0), axis=-1, keepdims=True) * (1.0 / MLA_NOPE)
        ss_r = jnp.sum(jnp.where(nope, 0.0, sq), axis=-1, keepdims=True) * (1.0 / MLA_ROPE)
        r = jnp.where(nope, lax.rsqrt(ss_n + EPS), lax.rsqrt(ss_r + EPS))
        q_ref[:, h * 128:(h + 1) * 128] = _rope_lanes(qh * r * qn_ref[...], cos, sin, lane).astype(BF16)


def mla_rows(x, norm_g, wd_pad, q_lora_norm, kv_lora_norm, krn_pad, wuq_pad, qn_pad, cos_t, sin_t, tm=512):
    M, D = x.shape
    tm = min(tm, M)
    const = lambda shape: pl.BlockSpec(shape, lambda i: (0,) * len(shape))
    rows = lambda n: pl.BlockSpec((tm, n), lambda i: (i, 0))
    return pl.pallas_call(
        _mla_rows_body,
        name="mla_rows",
        grid=(M // tm,),
        in_specs=[rows(D), const((1, D)), const(wd_pad.shape), const((1, MLA_Q_LORA)), const((1, MLA_KV_LORA)),
                  const((1, 128)), const(wuq_pad.shape), const((1, 128)), rows(128), rows(128)],
        out_specs=[rows(MLA_HEADS * 128), rows(MLA_KV_LORA), rows(128)],
        out_shape=[jax.ShapeDtypeStruct((M, MLA_HEADS * 128), BF16), jax.ShapeDtypeStruct((M, MLA_KV_LORA), F32),
                   jax.ShapeDtypeStruct((M, 128), F32)],
        compiler_params=_params("parallel"),
    )(x, norm_g.reshape(1, D), wd_pad, q_lora_norm.reshape(1, -1), kv_lora_norm.reshape(1, -1), krn_pad,
      wuq_pad, qn_pad, cos_t, sin_t)


def _mla_kv_body(ckv_ref, kr_ref, wuk_ref, wuv_ref, kn_ref, k_ref, v_ref):
    c = ckv_ref[...].astype(BF16)
    kn = jnp.dot(c, wuk_ref[...], preferred_element_type=F32)
    v_ref[...] = jnp.dot(c, wuv_ref[...], preferred_element_type=F32).astype(BF16)
    kr = kr_ref[...]
    for h in range(MLA_HEADS):
        kh = kn[:, h * 128:(h + 1) * 128]
        r = lax.rsqrt(jnp.sum(kh * kh, axis=-1, keepdims=True) * (1.0 / MLA_NOPE) + EPS)
        k_ref[:, h * 128:(h + 1) * 128] = (kh * r * kn_ref[...] + kr).astype(BF16)


def mla_kv(ckv, kr_pad, wuk_pad, wuv, kn_pad, tm=512):
    M = ckv.shape[0]
    tm = min(tm, M)
    const = lambda shape: pl.BlockSpec(shape, lambda i: (0,) * len(shape))
    rows = lambda n: pl.BlockSpec((tm, n), lambda i: (i, 0))
    return pl.pallas_call(
        _mla_kv_body,
        name="mla_kv",
        grid=(M // tm,),
        in_specs=[rows(MLA_KV_LORA), rows(128), const(wuk_pad.shape), const(wuv.shape), const((1, 128))],
        out_specs=[rows(MLA_HEADS * 128), rows(MLA_HEADS * MLA_V)],
        out_shape=[jax.ShapeDtypeStruct((M, MLA_HEADS * 128), BF16),
                   jax.ShapeDtypeStruct((M, MLA_HEADS * MLA_V), BF16)],
        compiler_params=_params("parallel"),
    )(ckv, kr_pad, wuk_pad, wuv, kn_pad)


def _flash_body(qi_ref, ki_ref, q_ref, k_ref, v_ref, *rest, causal):
    if causal:
        bias_ref, o_ref, m_ref, l_ref, acc_ref = rest
    else:
        o_ref, m_ref, l_ref, acc_ref = rest
    t = pl.program_id(2)
    qi = qi_ref[t]
    ki = ki_ref[t]
    tq = q_ref.shape[1]
    c2 = MLA_SCALE * math.log2(math.e)

    @pl.when(ki == 0)
    def _():
        m_ref[...] = jnp.full_like(m_ref, -jnp.inf)
        l_ref[...] = jnp.zeros_like(l_ref)
        acc_ref[...] = jnp.zeros_like(acc_ref)

    def step(masked):
        for h in range(2):
            q = q_ref[0, :, h * 128:(h + 1) * 128]
            k = k_ref[0, :, h * 128:(h + 1) * 128]
            s = lax.dot_general(q, k, NT_DIMS, preferred_element_type=F32)
            if masked:
                s = s + bias_ref[...]
            m_prev = m_ref[h]
            m_new = jnp.maximum(m_prev, jnp.max(s, axis=-1, keepdims=True))
            alpha = jnp.exp2((m_prev - m_new) * c2)
            p = jnp.exp2((s - m_new) * c2)
            l_ref[h] = alpha * l_ref[h] + jnp.sum(p, axis=-1, keepdims=True)
            acc_ref[h] = alpha * acc_ref[h] + jnp.dot(p.astype(BF16), v_ref[0], preferred_element_type=F32)
            m_ref[h] = m_new

    def finish():
        lane = lax.broadcasted_iota(jnp.int32, (tq, 128), 1)
        o = jnp.where(lane < MLA_V, acc_ref[0] / l_ref[0], acc_ref[1] / l_ref[1])
        o_ref[0] = o.astype(o_ref.dtype)

    if causal:
        @pl.when(ki < qi)
        def _():
            step(False)

        @pl.when(ki == qi)
        def _():
            step(True)
            finish()
    else:
        step(False)

        @pl.when(t == pl.num_programs(2) - 1)
        def _():
            finish()


def mla_attention(q, k, v, causal, tq, tk):
    B, Lq, _ = q.shape
    Lk = k.shape[1]
    nq, nk = Lq // tq, Lk // tk
    pairs = [(i, j) for i in range(nq) for j in range(nk) if (j <= i or not causal)]
    qi_tab = jnp.asarray([i for i, _ in pairs], jnp.int32)
    ki_tab = jnp.asarray([j for _, j in pairs], jnp.int32)
    in_specs = [pl.BlockSpec((1, tq, 256), lambda b, p, t, qt, kt: (b, qt[t], p)),
                pl.BlockSpec((1, tk, 256), lambda b, p, t, qt, kt: (b, kt[t], p)),
                pl.BlockSpec((1, tk, 128), lambda b, p, t, qt, kt: (b, kt[t], p))]
    args = [q, k, v]
    if causal:
        chunk = jnp.arange(tq) // CHUNK
        bias = jnp.where(chunk[None, :] <= chunk[:, None], 0.0, -jnp.inf).astype(F32)
        in_specs.append(pl.BlockSpec((tq, tk), lambda b, p, t, qt, kt: (0, 0)))
        args.append(bias)
    grid_spec = pltpu.PrefetchScalarGridSpec(
        num_scalar_prefetch=2,
        grid=(B, MLA_HEADS // 2, len(pairs)),
        in_specs=in_specs,
        out_specs=pl.BlockSpec((1, tq, 128), lambda b, p, t, qt, kt: (b, qt[t], p)),
        scratch_shapes=[pltpu.VMEM((2, tq, 1), F32), pltpu.VMEM((2, tq, 1), F32), pltpu.VMEM((2, tq, 128), F32)],
    )
    return pl.pallas_call(
        functools.partial(_flash_body, causal=causal),
        name="mla_flash",
        grid_spec=grid_spec,
        out_shape=jax.ShapeDtypeStruct((B, Lq, MLA_HEADS * MLA_V), BF16),
        compiler_params=_params("parallel", "parallel", "arbitrary"),
    )(qi_tab, ki_tab, *args)


def _rope_tables(pos):
    half = MLA_ROPE // 2
    inv = ROPE_THETA ** (-jnp.arange(half, dtype=F32) / half)
    ang = pos.astype(F32)[:, None] * inv[None, :]
    cos, sin = jnp.cos(ang), jnp.sin(ang)
    n = pos.shape[0]
    ones = jnp.ones((n, MLA_NOPE), F32)
    zeros = jnp.zeros((n, 128 - MLA_NOPE - MLA_ROPE), F32)
    cos_t = jnp.concatenate([ones, cos, cos, zeros], axis=1)
    sin_t = jnp.concatenate([jnp.zeros((n, MLA_NOPE), F32), -sin, sin, zeros], axis=1)
    return cos_t, sin_t


def _pad_heads(w, d_in, d_head, d_pad):
    w = w.reshape(d_in, MLA_HEADS, d_head)
    return jnp.pad(w, ((0, 0), (0, 0), (0, d_pad - d_head))).reshape(d_in, MLA_HEADS * d_pad)


def mla_weights(norm_g, w_down, q_lora_norm, kv_lora_norm, w_uq, w_ukv, q_nope_norm, q_rope_norm, k_nope_norm,
                k_rope_norm, w_o):
    D = w_down.shape[0]
    split = MLA_Q_LORA + MLA_KV_LORA
    tail = 128 - MLA_NOPE - MLA_ROPE
    wd_pad = jnp.concatenate([w_down[:, :split], jnp.zeros((D, MLA_NOPE), F32), w_down[:, split:],
                              jnp.zeros((D, tail), F32)], axis=1).astype(BF16)
    lane_pad = lambda a, b: jnp.concatenate([a, b, jnp.zeros((128 - a.shape[0] - b.shape[0],), F32)]).reshape(1, 128)
    krn_pad = lane_pad(jnp.zeros((MLA_NOPE,), F32), k_rope_norm)
    qn_pad = lane_pad(q_nope_norm, q_rope_norm)
    kn_pad = lane_pad(k_nope_norm, jnp.zeros((0,), F32))
    wuq_pad = _pad_heads(w_uq, MLA_Q_LORA, MLA_NOPE + MLA_ROPE, 128).astype(BF16)
    wukv = w_ukv.reshape(MLA_KV_LORA, MLA_HEADS, MLA_NOPE + MLA_V)
    wuk_pad = _pad_heads(wukv[:, :, :MLA_NOPE].reshape(MLA_KV_LORA, -1), MLA_KV_LORA, MLA_NOPE, 128).astype(BF16)
    wuv = wukv[:, :, MLA_NOPE:].reshape(MLA_KV_LORA, MLA_HEADS * MLA_V).astype(BF16)
    rows_w = (norm_g, wd_pad, q_lora_norm, kv_lora_norm, krn_pad, wuq_pad, qn_pad)
    return rows_w, (wuk_pad, wuv, kn_pad), w_o.astype(BF16)


def mla_block(x, b, l, mw, ckv_past, kr_past, tile):
    rows_w, kv_w, w_o = mw
    n_past = 0 if ckv_past is None else ckv_past.shape[1]
    cos_t, sin_t = _rope_tables(n_past + jnp.arange(l))
    q, ckv, kr = mla_rows(x, *rows_w, jnp.tile(cos_t, (b, 1)), jnp.tile(sin_t, (b, 1)))
    kr_out = kr[:, MLA_NOPE:MLA_NOPE + MLA_ROPE]
    if ckv_past is None:
        k_all, v_all = mla_kv(ckv, kr, *kv_w)
        o = mla_attention(q.reshape(b, l, -1), k_all.reshape(b, l, -1), v_all.reshape(b, l, -1), True, tile, tile)
    else:
        lk = n_past + l
        ckv_all = jnp.concatenate([ckv_past, ckv.reshape(b, l, -1)], axis=1).reshape(b * lk, -1)
        kr_pad = jnp.pad(kr_past, ((0, 0), (0, 0), (MLA_NOPE, 128 - MLA_NOPE - MLA_ROPE)))
        kr_all = jnp.concatenate([kr_pad, kr.reshape(b, l, 128)], axis=1).reshape(b * lk, 128)
        k_all, v_all = mla_kv(ckv_all, kr_all, *kv_w, tm=lk // 5)
        o = mla_attention(q.reshape(b, l, -1), k_all.reshape(b, lk, -1), v_all.reshape(b, lk, -1), False, l, lk)
    x = rms_matmul(o.reshape(b * l, -1), w_o, res=x, tm=1024, name="mla_out_proj")
    return x, ckv, kr_out


def kernel(x_prompt, x_sample, mem_prompt, state_conv_a, state_conv_ssm, state_ssm, cache_mla_ckv, cache_mla_krope, cache_mem_k, cache_mem_v, norm_mix, norm_mem, norm_ffn, w_in_e, conv_a_w, conv_s_w, conv_s_b, dt_bias, a_log, d_skip, ssm_norm, w_out_e, w_down_o, q_lora_norm, kv_lora_norm, w_uq, w_ukv, q_nope_norm, q_rope_norm, k_nope_norm, k_rope_norm, w_o_mla, mem_src_norm, w_mem_q, w_mem_k, w_mem_v, mem_q_norm, mem_k_norm, w_mem_o, w_ffn1, w_ffn3, w_ffn2, w_router, w_exp1, w_exp3, w_exp2):
    bp, lp, D = x_prompt.shape
    bs, ls, _ = x_sample.shape
    past = cache_mla_ckv.shape[2]
    bf = lambda w: w.astype(BF16)
    xp = x_prompt.reshape(bp * lp, D)
    xs = x_sample.reshape(bs * ls, D)

    ew = even_weights(w_in_e[0], w_out_e[0], norm_mix[0], conv_a_w[0], conv_s_w[0], conv_s_b[0], dt_bias[0],
                      a_log[0], d_skip[0], ssm_norm[0])
    xp, p_ca, p_cs, p_st = even_mixer_block(
        xp, bp, lp, ew, jnp.zeros((bp, 2, A_WIDTH), F32), jnp.zeros((bp, 3, SSM_XBC), F32),
        jnp.zeros((bp, SSM_HEADS, SSM_HEAD_DIM, SSM_STATE), F32), 1024)
    xs, s_ca, s_cs, s_st = even_mixer_block(xs, bs, ls, ew, state_conv_a[0], state_conv_ssm[0], state_ssm[0], 512)

    p_mk, p_mv = [], []

    def memory_block(i, xp, xs):
        mk, mv, mk_b, mv_b = mem_kv(mem_prompt, mem_src_norm[i], bf(w_mem_k[i]), bf(w_mem_v[i]), mem_k_norm[i])
        p_mk.append(mk.reshape(bp, MEM_TOKENS, MEM_HEADS, MEM_HEAD_DIM))
        p_mv.append(mv.reshape(bp, MEM_TOKENS, MEM_HEADS, MEM_HEAD_DIM))
        wq, wo = bf(w_mem_q[i]), bf(w_mem_o[i])
        xp = mem_attend(xp, norm_mem[i], mk_b, mv_b, wq, mem_q_norm[i], wo, lp)
        ck = bf(cache_mem_k[i]).reshape(bs, MEM_TOKENS, D)
        cv = bf(cache_mem_v[i]).reshape(bs, MEM_TOKENS, D)
        xs = mem_attend(xs, norm_mem[i], ck, cv, wq, mem_q_norm[i], wo, ls)
        return xp, xs

    xp, xs = memory_block(0, xp, xs)
    w1, w3, w2 = bf(w_ffn1[0]), bf(w_ffn3[0]), bf(w_ffn2[0])
    xp = ffn_dense(xp, norm_ffn[0], w1, w3, w2)
    xs = ffn_dense(xs, norm_ffn[0], w1, w3, w2)

    mw = mla_weights(norm_mix[1], w_down_o[0], q_lora_norm[0], kv_lora_norm[0], w_uq[0], w_ukv[0], q_nope_norm[0],
                     q_rope_norm[0], k_nope_norm[0], k_rope_norm[0], w_o_mla[0])
    xp, p_ckv, p_kr = mla_block(xp, bp, lp, mw, None, None, min(lp, 1024))
    xs, s_ckv, s_kr = mla_block(xs, bs, ls, mw, cache_mla_ckv[0], cache_mla_krope[0], None)

    xp, xs = memory_block(1, xp, xs)
    we1, we3, we2 = bf(w_exp1[0]), bf(w_exp3[0]), bf(w_exp2[0])
    xp, xs = moe_ffn([xp, xs], norm_ffn[1], w_router[0], we1, we3, we2, 512)

    return (xp.reshape(bp, lp, D), xs.reshape(bs, ls, D),
            p_ca[None], p_cs[None], p_st[None],
            p_ckv.reshape(1, bp, lp, -1), p_kr.reshape(1, bp, lp, -1), jnp.stack(p_mk), jnp.stack(p_mv),
            s_ca[None], s_cs[None], s_st[None],
            s_ckv.reshape(1, bs, ls, -1), s_kr.reshape(1, bs, ls, -1))
```

```python
import functools
import math

import jax
import jax.numpy as jnp
from jax import lax
from jax.experimental import pallas as pl
from jax.experimental.pallas import tpu as pltpu

F32 = jnp.float32
BF16 = jnp.bfloat16
EPS = 1e-6

D_MODEL = 1024
CHUNK = 64
A_WIDTH = 1024
SSM_HEADS = 32
SSM_HEAD_DIM = 64
SSM_INNER = 2048
SSM_STATE = 128
SSM_GROUPS = 4
SSM_XBC = 3072
HEAD_LANES = 128
PROJ_W = 3 * A_WIDTH + SSM_INNER + SSM_XBC + HEAD_LANES
COL_Z = 3 * A_WIDTH
COL_XBC = COL_Z + SSM_INNER
COL_DT = COL_XBC + SSM_XBC
MLA_HEADS = 16
MLA_Q_LORA = 512
MLA_KV_LORA = 256
MLA_NOPE = 64
MLA_ROPE = 32
MLA_V = 64
MLA_SCALE = (MLA_NOPE + MLA_ROPE) ** -0.5
ROPE_THETA = 10000.0
MEM_TOKENS = 256
MEM_HEADS = 4
MEM_HEAD_DIM = 256
N_EXPERTS = 8
VMEM_LIMIT = 56 * 1024 * 1024
MIXER_CHUNKS_PER_STEP = 2
FLASH_KV_SUB = 256
NT_DIMS = (((1,), (1,)), ((), ()))


def _params(*sem):
    return pltpu.CompilerParams(dimension_semantics=sem, vmem_limit_bytes=VMEM_LIMIT)


def _rms(x, g):
    return x * lax.rsqrt(jnp.mean(x * x, axis=-1, keepdims=True) + EPS) * g


def _split3(x):
    hi = x.astype(BF16)
    r1 = x - hi.astype(F32)
    mid = r1.astype(BF16)
    lo = (r1 - mid.astype(F32)).astype(BF16)
    return hi, mid, lo


def _dot_sel(x, sel_bf16):
    hi, mid, lo = _split3(x)
    d = lambda a: jnp.dot(a, sel_bf16, preferred_element_type=F32)
    return d(hi) + d(mid) + d(lo)


def _dot_sel2(x, sel_bf16):
    hi = x.astype(BF16)
    mid = (x - hi.astype(F32)).astype(BF16)
    d = lambda a: jnp.dot(a, sel_bf16, preferred_element_type=F32)
    return d(hi) + d(mid)


def _sel_dot_nt(sel_bf16, x):
    hi, mid, lo = _split3(x)
    d = lambda a: lax.dot_general(sel_bf16, a, NT_DIMS, preferred_element_type=F32)
    return d(hi) + d(mid) + d(lo)


def _sel_dot(sel_bf16, x):
    hi, mid, lo = _split3(x)
    d = lambda a: jnp.dot(sel_bf16, a, preferred_element_type=F32)
    return d(hi) + d(mid) + d(lo)


def _rms_matmul_body(*refs, use_norm, use_res):
    it = iter(refs)
    x_ref = next(it)
    g_ref = next(it) if use_norm else None
    w_ref = next(it)
    r_ref = next(it) if use_res else None
    o_ref = next(it)
    xn_ref = next(it) if use_norm else None
    if use_norm:
        @pl.when(pl.program_id(1) == 0)
        def _():
            xn_ref[...] = _rms(x_ref[...], g_ref[...]).astype(BF16)
        xb = xn_ref[...]
    else:
        xb = x_ref[...]
    acc = jnp.dot(xb, w_ref[...], preferred_element_type=F32)
    if use_res:
        acc = acc + r_ref[...]
    o_ref[...] = acc.astype(o_ref.dtype)


def rms_matmul(x, w, g=None, res=None, out_dtype=F32, tm=512, tn=None, name="rms_matmul"):
    M, K = x.shape
    N = w.shape[1]
    tn = tn or N
    tm = min(tm, M)
    in_specs = [pl.BlockSpec((tm, K), lambda i, j: (i, 0))]
    args = [x]
    if g is not None:
        in_specs.append(pl.BlockSpec((1, K), lambda i, j: (0, 0)))
        args.append(g.reshape(1, K))
    in_specs.append(pl.BlockSpec((K, tn), lambda i, j: (0, j)))
    args.append(w)
    if res is not None:
        in_specs.append(pl.BlockSpec((tm, tn), lambda i, j: (i, j)))
        args.append(res)
    scratch = [pltpu.VMEM((tm, K), BF16)] if g is not None else []
    return pl.pallas_call(
        functools.partial(_rms_matmul_body, use_norm=g is not None, use_res=res is not None),
        name=name,
        grid=(M // tm, N // tn),
        in_specs=in_specs,
        out_specs=pl.BlockSpec((tm, tn), lambda i, j: (i, j)),
        out_shape=jax.ShapeDtypeStruct((M, N), out_dtype),
        scratch_shapes=scratch,
        compiler_params=_params("parallel", "arbitrary"),
    )(*args)


def _softplus(v):
    return jnp.maximum(v, 0.0) + jnp.log1p(jnp.exp(-jnp.abs(v)))


def _mixer_body(proj_ref, wa_ref, ws_ref, bs_ref, dtb_ref, alog_ref, dskip_ref, gn_ref,
                tri_ref, selrow_ref, expand_ref, eye_ref,
                ca_prev_ref, cs_prev_ref, st_prev_ref,
                mix_ref, ca_out_ref, cs_out_ref, st_ref,
                abuf, sbuf):
    c = pl.program_id(1)
    Q = CHUNK
    R = proj_ref.shape[1]

    @pl.when(c == 0)
    def _():
        abuf[0:8, :] = ca_prev_ref[0]
        sbuf[0:8, :] = cs_prev_ref[0]
        st_ref[0] = st_prev_ref[0]

    g_b = proj_ref[0, :, 0:A_WIDTH]
    abuf[8:8 + R, :] = proj_ref[0, :, A_WIDTH:2 * A_WIDTH] * proj_ref[0, :, 2 * A_WIDTH:3 * A_WIDTH]
    conv_a = wa_ref[0:1, :] * abuf[6:6 + R, :]
    conv_a = conv_a + wa_ref[1:2, :] * abuf[7:7 + R, :]
    conv_a = conv_a + wa_ref[2:3, :] * abuf[8:8 + R, :]
    mix_ref[0, :, 0:A_WIDTH] = (g_b * conv_a).astype(BF16)
    tail_a = abuf[R:R + 8, :]
    ca_out_ref[0] = tail_a
    abuf[0:8, :] = tail_a

    sbuf[8:8 + R, :] = proj_ref[0, :, COL_XBC:COL_DT]
    xc_all = ws_ref[0:1, :] * sbuf[5:5 + R, :]
    xc_all = xc_all + ws_ref[1:2, :] * sbuf[6:6 + R, :]
    xc_all = xc_all + ws_ref[2:3, :] * sbuf[7:7 + R, :]
    xc_all = xc_all + ws_ref[3:4, :] * sbuf[8:8 + R, :]
    xc_all = xc_all + bs_ref[...]
    xc_all = xc_all * jax.nn.sigmoid(xc_all)
    tail_s = sbuf[R:R + 8, :]
    cs_out_ref[0] = tail_s
    sbuf[0:8, :] = tail_s

    dt_all = _softplus(proj_ref[0, :, COL_DT:COL_DT + HEAD_LANES] + dtb_ref[...])
    la_all = dt_all * (-jnp.exp(alog_ref[...]))

    lane = lax.broadcasted_iota(jnp.int32, (Q, 128), 1)
    row = lax.broadcasted_iota(jnp.int32, (Q, 128), 0)
    tril2 = jnp.where(lane >= Q, lane - Q, lane) <= row
    left = lane < Q
    eye = eye_ref[...]

    for ci in range(R // Q):
        _ssd_chunk(slice(ci * Q, (ci + 1) * Q), xc_all, dt_all, la_all, tril2, left, eye,
                   proj_ref, dskip_ref, gn_ref, tri_ref, selrow_ref, expand_ref, mix_ref, st_ref)


def _ssd_chunk(rows, xc_all, dt_all, la_all, tril2, left, eye,
               proj_ref, dskip_ref, gn_ref, tri_ref, selrow_ref, expand_ref, mix_ref, st_ref):
    Q = CHUNK
    xc = xc_all[rows]
    dt = dt_all[rows]
    cs = _sel_dot(tri_ref[...], la_all[rows])
    cs_last = cs[Q - 1:Q, :]
    w_st = dt * jnp.exp(cs_last - cs)
    cs_x = _dot_sel(cs, expand_ref[...])
    w_st_x = _dot_sel2(w_st, expand_ref[...])
    e_in_x = jnp.exp(cs_x)
    dec_x = e_in_x[Q - 1:Q, :]
    cs_r = _sel_dot_nt(selrow_ref[...], cs)
    dt_r = _sel_dot_nt(selrow_ref[...], dt)
    cs_rp = jnp.concatenate([cs_r[0:16], cs_r[16:32]], axis=1)
    dt_rp = jnp.concatenate([dt_r[0:16], dt_r[16:32]], axis=1)

    for g in range(SSM_GROUPS):
        gl = slice(SSM_INNER + g * SSM_STATE, SSM_INNER + (g + 1) * SSM_STATE)
        Bg = xc[:, gl].astype(BF16)
        Cg = xc[:, gl.start + SSM_GROUPS * SSM_STATE: gl.stop + SSM_GROUPS * SSM_STATE].astype(BF16)
        sc = lax.dot_general(Cg, Bg, NT_DIMS, preferred_element_type=F32)
        sc2 = jnp.concatenate([sc, sc], axis=1)
        BgT = lax.dot_general(eye, Bg, NT_DIMS, preferred_element_type=F32).astype(BF16)
        hs = slice(g * 512, (g + 1) * 512)
        st_g = st_ref[0, :, hs]
        y_off = jnp.dot(Cg, st_g.astype(BF16), preferred_element_type=F32)
        xw = (xc[:, hs] * w_st_x[:, hs]).astype(BF16)
        st_ref[0, :, hs] = dec_x[:, hs] * st_g + jnp.dot(BgT, xw, preferred_element_type=F32)
        ys = []
        for kk in range(4):
            k = 4 * g + kk
            pl_ = slice(k * 128, (k + 1) * 128)
            diff = cs_x[:, pl_] - cs_rp[k:k + 1, :]
            decay = jnp.exp(jnp.where(tril2, diff, -jnp.inf))
            m_pair = (sc2 * decay * dt_rp[k:k + 1, :]).astype(BF16)
            xp = xc[:, pl_]
            rhs = jnp.concatenate([jnp.where(left, xp, 0.0), jnp.where(left, 0.0, xp)], axis=0).astype(BF16)
            y_diag = jnp.dot(m_pair, rhs, preferred_element_type=F32)
            y = y_diag + y_off[:, kk * 128:(kk + 1) * 128] * e_in_x[:, pl_]
            y = y + dskip_ref[:, pl_] * xp
            z = proj_ref[0, rows, COL_Z + k * 128: COL_Z + (k + 1) * 128]
            ys.append(y * (z * jax.nn.sigmoid(z)))
        yg = jnp.concatenate(ys, axis=1)
        yn = yg * lax.rsqrt(jnp.mean(yg * yg, axis=-1, keepdims=True) + EPS) * gn_ref[:, hs]
        mix_ref[0, rows, A_WIDTH + g * 512: A_WIDTH + (g + 1) * 512] = yn.astype(BF16)


def mixer_even(proj, conv_a_w, conv_s_w, conv_s_b, dt_bias, a_log, d_skip, ssm_norm,
               conv_a_prev8, conv_s_prev8, ssm_prev_t):
    B, L, _ = proj.shape
    rows = CHUNK * MIXER_CHUNKS_PER_STEP if L % (CHUNK * MIXER_CHUNKS_PER_STEP) == 0 else CHUNK
    pad_h =lambda v: jnp.pad(v.reshape(1, SSM_HEADS), ((0, 0), (0, HEAD_LANES - SSM_HEADS)))
    tri = (jnp.arange(CHUNK)[:, None] >= jnp.arange(CHUNK)[None, :]).astype(BF16)
    heads = jnp.arange(HEAD_LANES)
    order = jnp.concatenate([jnp.arange(0, SSM_HEADS, 2), jnp.arange(1, SSM_HEADS, 2)])
    selrow = (order[:, None] == heads[None, :]).astype(BF16)
    expand = (heads[:, None] == (jnp.arange(SSM_INNER) // SSM_HEAD_DIM)[None, :]).astype(BF16)
    eye = jnp.eye(SSM_STATE, dtype=BF16)
    dskip_x = jnp.repeat(d_skip.astype(F32), SSM_HEAD_DIM).reshape(1, SSM_INNER)
    const = lambda shape: pl.BlockSpec(shape, lambda b, c: (0,) * len(shape))
    per_b = lambda shape: pl.BlockSpec((1,) + shape, lambda b, c: (b,) + (0,) * len(shape))
    return pl.pallas_call(
        _mixer_body,
        name="ssd_mixer",
        grid=(B, L // rows),
        in_specs=[
            pl.BlockSpec((1, rows, PROJ_W), lambda b, c: (b, c, 0)),
            const((3, A_WIDTH)), const((4, SSM_XBC)), const((1, SSM_XBC)),
            const((1, HEAD_LANES)), const((1, HEAD_LANES)), const((1, SSM_INNER)), const((1, SSM_INNER)),
            const((CHUNK, CHUNK)), const((SSM_HEADS, HEAD_LANES)), const((HEAD_LANES, SSM_INNER)),
            const((SSM_STATE, SSM_STATE)),
            per_b((8, A_WIDTH)), per_b((8, SSM_XBC)), per_b((SSM_STATE, SSM_INNER)),
        ],
        out_specs=[
            pl.BlockSpec((1, rows, A_WIDTH + SSM_INNER), lambda b, c: (b, c, 0)),
            per_b((8, A_WIDTH)), per_b((8, SSM_XBC)), per_b((SSM_STATE, SSM_INNER)),
        ],
        out_shape=[
            jax.ShapeDtypeStruct((B, L, A_WIDTH + SSM_INNER), BF16),
            jax.ShapeDtypeStruct((B, 8, A_WIDTH), F32),
            jax.ShapeDtypeStruct((B, 8, SSM_XBC), F32),
            jax.ShapeDtypeStruct((B, SSM_STATE, SSM_INNER), F32),
        ],
        scratch_shapes=[
            pltpu.VMEM((rows + 8, A_WIDTH), F32),
            pltpu.VMEM((rows + 8, SSM_XBC), F32),
        ],
        compiler_params=_params("parallel", "arbitrary"),
    )(proj, conv_a_w, conv_s_w, conv_s_b.reshape(1, SSM_XBC), pad_h(dt_bias), pad_h(a_log), dskip_x,
      ssm_norm.reshape(1, SSM_INNER), tri, selrow, expand, eye, conv_a_prev8, conv_s_prev8, ssm_prev_t)


def even_weights(w_in, w_out, norm_g, conv_a_w, conv_s_w, conv_s_b, dt_bias, a_log, d_skip, ssm_norm):
    w_in_pad = jnp.pad(w_in, ((0, 0), (0, PROJ_W - w_in.shape[1]))).astype(BF16)
    return (norm_g, w_in_pad, w_out.astype(BF16), (conv_a_w, conv_s_w, conv_s_b, dt_bias, a_log, d_skip, ssm_norm))


def even_mixer_block(x, b, l, ew, ca_prev, cs_prev, st_prev, tm):
    norm_g, w_in_pad, w_out, mixer_w = ew
    proj = rms_matmul(x, w_in_pad, g=norm_g, tm=tm, tn=PROJ_W // 5, name="mixer_in_proj")
    ca8 = jnp.pad(ca_prev, ((0, 0), (6, 0), (0, 0)))
    cs8 = jnp.pad(cs_prev, ((0, 0), (5, 0), (0, 0)))
    st_t = jnp.transpose(st_prev, (0, 3, 1, 2)).reshape(b, SSM_STATE, SSM_INNER)
    mix, ca, cs, st = mixer_even(proj.reshape(b, l, PROJ_W), *mixer_w, ca8, cs8, st_t)
    x = rms_matmul(mix.reshape(b * l, -1), w_out, res=x, tm=tm, name="mixer_out_proj")
    st = jnp.transpose(st.reshape(b, SSM_STATE, SSM_HEADS, SSM_HEAD_DIM), (0, 2, 3, 1))
    return x, ca[:, 6:8], cs[:, 5:8], st


def _mem_kv_body(mem_ref, g_ref, wk_ref, wv_ref, kn_ref, k_ref, v_ref, kb_ref, vb_ref):
    mm = _rms(mem_ref[0], g_ref[...]).astype(BF16)
    k = jnp.dot(mm, wk_ref[...], preferred_element_type=F32)
    v = jnp.dot(mm, wv_ref[...], preferred_element_type=F32)
    for h in range(MEM_HEADS):
        sl = slice(h * MEM_HEAD_DIM, (h + 1) * MEM_HEAD_DIM)
        kh = _rms(k[:, sl], kn_ref[...])
        k_ref[0, :, sl] = kh
        kb_ref[0, :, sl] = kh.astype(BF16)
    v_ref[0] = v
    vb_ref[0] = v.astype(BF16)


def mem_kv(mem, src_norm, w_k, w_v, k_norm):
    B, Mt, D = mem.shape
    const = lambda shape: pl.BlockSpec(shape, lambda b: (0,) * len(shape))
    blk = pl.BlockSpec((1, Mt, D), lambda b: (b, 0, 0))
    return pl.pallas_call(
        _mem_kv_body,
        name="mem_kv",
        grid=(B,),
        in_specs=[blk, const((1, D)), const((D, D)), const((D, D)), const((1, MEM_HEAD_DIM))],
        out_specs=[blk, blk, blk, blk],
        out_shape=[jax.ShapeDtypeStruct((B, Mt, D), F32), jax.ShapeDtypeStruct((B, Mt, D), F32),
                   jax.ShapeDtypeStruct((B, Mt, D), BF16), jax.ShapeDtypeStruct((B, Mt, D), BF16)],
        compiler_params=_params("parallel"),
    )(mem, src_norm.reshape(1, D), w_k, w_v, k_norm.reshape(1, MEM_HEAD_DIM))


def _mem_attn_body(x_ref, g_ref, wq_ref, qn_ref, k_ref, v_ref, wo_ref, o_ref, *, nb):
    x = x_ref[...]
    tm = x.shape[0]
    rows = tm // nb
    hn = _rms(x, g_ref[...]).astype(BF16)
    q = jnp.dot(hn, wq_ref[...], preferred_element_type=F32)
    scale = MEM_HEAD_DIM ** -0.5
    outs = []
    for h in range(MEM_HEADS):
        sl = slice(h * MEM_HEAD_DIM, (h + 1) * MEM_HEAD_DIM)
        qh = _rms(q[:, sl], qn_ref[...]).astype(BF16)
        parts = []
        for s in range(nb):
            qs = qh[s * rows:(s + 1) * rows]
            sc = lax.dot_general(qs, k_ref[s, :, sl], NT_DIMS, preferred_element_type=F32) * scale
            sc = sc - jnp.max(sc, axis=-1, keepdims=True)
            p = jnp.exp(sc)
            p = p / jnp.sum(p, axis=-1, keepdims=True)
            parts.append(jnp.dot(p.astype(BF16), v_ref[s, :, sl], preferred_element_type=F32))
        outs.append(parts[0] if nb == 1 else jnp.concatenate(parts, axis=0))
    o = jnp.concatenate(outs, axis=1).astype(BF16)
    o_ref[...] = x + jnp.dot(o, wo_ref[...], preferred_element_type=F32)


def mem_attend(x, norm_g, k_b, v_b, w_q, q_norm, w_o, rows_per_seq, tm=512):
    M, D = x.shape
    tm = min(tm, M)
    nb = max(1, tm // rows_per_seq)
    tiles_per_seq = max(1, rows_per_seq // tm)
    const = lambda shape: pl.BlockSpec(shape, lambda i: (0,) * len(shape))
    kv_spec = pl.BlockSpec((nb, MEM_TOKENS, D), lambda i: (i // tiles_per_seq, 0, 0))
    return pl.pallas_call(
        functools.partial(_mem_attn_body, nb=nb),
        name="mem_attn",
        grid=(M // tm,),
        in_specs=[pl.BlockSpec((tm, D), lambda i: (i, 0)), const((1, D)), const((D, D)),
                  const((1, MEM_HEAD_DIM)), kv_spec, kv_spec, const((D, D))],
        out_specs=pl.BlockSpec((tm, D), lambda i: (i, 0)),
        out_shape=jax.ShapeDtypeStruct((M, D), F32),
        compiler_params=_params("parallel"),
    )(x, norm_g.reshape(1, D), w_q, q_norm.reshape(1, MEM_HEAD_DIM), k_b, v_b, w_o)


def _swiglu_step(xn, w1, w3, w2):
    h1 = jnp.dot(xn, w1, preferred_element_type=F32)
    h3 = jnp.dot(xn, w3, preferred_element_type=F32)
    a = (h1 * jax.nn.sigmoid(h1) * h3).astype(BF16)
    return jnp.dot(a, w2, preferred_element_type=F32)


def _ffn_dense_body(x_ref, g_ref, w1_ref, w3_ref, w2_ref, o_ref, xn_ref, acc_ref):
    f = pl.program_id(1)

    @pl.when(f == 0)
    def _():
        xn_ref[...] = _rms(x_ref[...], g_ref[...]).astype(BF16)
        acc_ref[...] = jnp.zeros_like(acc_ref)

    acc_ref[...] += _swiglu_step(xn_ref[...], w1_ref[...], w3_ref[...], w2_ref[...])

    @pl.when(f == pl.num_programs(1) - 1)
    def _():
        o_ref[...] = x_ref[...] + acc_ref[...]


def ffn_dense(x, norm_g, w1, w3, w2, tm=512, tf=1408):
    M, D = x.shape
    F = w1.shape[1]
    tm = min(tm, M)
    return pl.pallas_call(
        _ffn_dense_body,
        name="ffn_dense",
        grid=(M // tm, F // tf),
        in_specs=[pl.BlockSpec((tm, D), lambda i, f: (i, 0)), pl.BlockSpec((1, D), lambda i, f: (0, 0)),
                  pl.BlockSpec((D, tf), lambda i, f: (0, f)), pl.BlockSpec((D, tf), lambda i, f: (0, f)),
                  pl.BlockSpec((tf, D), lambda i, f: (f, 0))],
        out_specs=pl.BlockSpec((tm, D), lambda i, f: (i, 0)),
        out_shape=jax.ShapeDtypeStruct((M, D), F32),
        scratch_shapes=[pltpu.VMEM((tm, D), BF16), pltpu.VMEM((tm, D), F32)],
        compiler_params=_params("parallel", "arbitrary"),
    )(x, norm_g.reshape(1, D), w1, w3, w2)


def _ffn_expert_body(te_ref, tv_ref, x_ref, w1_ref, w3_ref, w2_ref, o_ref, xn_ref, acc_ref):
    i = pl.program_id(0)
    f = pl.program_id(1)
    last = f == pl.num_programs(1) - 1

    @pl.when(tv_ref[i] > 0)
    def _():
        @pl.when(f == 0)
        def _():
            xn_ref[...] = x_ref[...].astype(BF16)
            acc_ref[...] = jnp.zeros_like(acc_ref)

        acc_ref[...] += _swiglu_step(xn_ref[...], w1_ref[0], w3_ref[0], w2_ref[0])

        @pl.when(last)
        def _():
            o_ref[...] = acc_ref[...]

    @pl.when(jnp.logical_and(tv_ref[i] == 0, last))
    def _():
        o_ref[...] = jnp.zeros_like(o_ref)


def ffn_experts(xs, tile_expert, tile_valid, w1, w3, w2, tm, tf=1792):
    N, D = xs.shape
    F = w1.shape[2]
    grid_spec = pltpu.PrefetchScalarGridSpec(
        num_scalar_prefetch=2,
        grid=(N // tm, F // tf),
        in_specs=[pl.BlockSpec((tm, D), lambda i, f, te, tv: (i, 0)),
                  pl.BlockSpec((1, D, tf), lambda i, f, te, tv: (te[i], 0, f)),
                  pl.BlockSpec((1, D, tf), lambda i, f, te, tv: (te[i], 0, f)),
                  pl.BlockSpec((1, tf, D), lambda i, f, te, tv: (te[i], f, 0))],
        out_specs=pl.BlockSpec((tm, D), lambda i, f, te, tv: (i, 0)),
        scratch_shapes=[pltpu.VMEM((tm, D), BF16), pltpu.VMEM((tm, D), F32)],
    )
    return pl.pallas_call(
        _ffn_expert_body,
        name="ffn_experts",
        grid_spec=grid_spec,
        out_shape=jax.ShapeDtypeStruct((N, D), F32),
        compiler_params=_params("arbitrary", "arbitrary"),
    )(tile_expert, tile_valid, xs, w1, w3, w2)


def _router_body(x_ref, g_ref, wr_ref, hn_ref, route_ref):
    hn = _rms(x_ref[...], g_ref[...])
    hn_ref[...] = hn
    xh, xm, xl = _split3(hn)
    wh, wm, wl = wr_ref[0], wr_ref[1], wr_ref[2]
    d = lambda a, b: jnp.dot(a, b, preferred_element_type=F32)
    logits = d(xh, wh) + (d(xh, wm) + d(xm, wh)) + (d(xm, wm) + d(xh, wl) + d(xl, wh))
    lane = lax.broadcasted_iota(jnp.int32, logits.shape, 1)
    lane_f = lane.astype(F32)
    neg = -jnp.inf
    lg = jnp.where(lane < N_EXPERTS, logits, neg)
    m1 = jnp.max(lg, axis=-1, keepdims=True)
    i1 = jnp.min(jnp.where(lg == m1, lane_f, 128.0), axis=-1, keepdims=True)
    lg2 = jnp.where(lane_f == i1, neg, lg)
    m2 = jnp.max(lg2, axis=-1, keepdims=True)
    i2 = jnp.min(jnp.where(lg2 == m2, lane_f, 128.0), axis=-1, keepdims=True)
    e = jnp.exp(m2 - m1)
    den = 1.0 + e
    g1 = 1.0 / den
    g2 = e / den
    route = jnp.where(lane == 0, g1, jnp.where(lane == 1, g2, jnp.where(lane == 2, i1, jnp.where(lane == 3, i2, 0.0))))
    route_ref[...] = route


def router(x, norm_g, w_router, tm=512):
    M, D = x.shape
    tm = min(tm, M)
    wr = jnp.pad(w_router.astype(F32), ((0, 0), (0, 128 - N_EXPERTS)))
    wh = wr.astype(BF16)
    r1 = wr - wh.astype(F32)
    wm = r1.astype(BF16)
    wl = (r1 - wm.astype(F32)).astype(BF16)
    w3 = jnp.stack([wh, wm, wl])
    return pl.pallas_call(
        _router_body,
        name="router",
        grid=(M // tm,),
        in_specs=[pl.BlockSpec((tm, D), lambda i: (i, 0)), pl.BlockSpec((1, D), lambda i: (0, 0)),
                  pl.BlockSpec((3, D, 128), lambda i: (0, 0, 0))],
        out_specs=[pl.BlockSpec((tm, D), lambda i: (i, 0)), pl.BlockSpec((tm, 128), lambda i: (i, 0))],
        out_shape=[jax.ShapeDtypeStruct((M, D), F32), jax.ShapeDtypeStruct((M, 128), F32)],
        compiler_params=_params("parallel"),
    )(x, norm_g.reshape(1, D), w3)


def _row_put(src_ref, r, dst_hbm, idx, sem):
    return pltpu.make_async_copy(src_ref.at[pl.ds(r, 1), :], dst_hbm.at[pl.ds(idx, 1), :], sem)


def _row_get(src_hbm, idx, dst_ref, r, sem):
    return pltpu.make_async_copy(src_hbm.at[pl.ds(idx, 1), :], dst_ref.at[pl.ds(r, 1), :], sem)


def _dispatch_body(i1_ref, i2_ref, hn_ref, xs_in_hbm, xs_hbm, sem):
    del xs_in_hbm
    tm = hn_ref.shape[0]

    def start(r, _):
        _row_put(hn_ref, r, xs_hbm, i1_ref[0, 0, r], sem).start()
        _row_put(hn_ref, r, xs_hbm, i2_ref[0, 0, r], sem).start()
        return 0

    lax.fori_loop(0, tm, start, 0, unroll=8)

    def wait(r, _):
        _row_put(hn_ref, r, xs_hbm, 0, sem).wait()
        _row_put(hn_ref, r, xs_hbm, 0, sem).wait()
        return 0

    lax.fori_loop(0, tm, wait, 0, unroll=8)


def dispatch_rows(hn, pos1, pos2, xs, tm=512):
    M, D = hn.shape
    tm = min(tm, M)
    idx_spec = pl.BlockSpec((1, 1, tm), lambda i: (i, 0, 0), memory_space=pltpu.SMEM)
    return pl.pallas_call(
        _dispatch_body,
        name="dispatch_rows",
        grid=(M // tm,),
        in_specs=[idx_spec, idx_spec, pl.BlockSpec((tm, D), lambda i: (i, 0)), pl.BlockSpec(memory_space=pl.ANY)],
        out_specs=pl.BlockSpec(memory_space=pl.ANY),
        out_shape=jax.ShapeDtypeStruct(xs.shape, xs.dtype),
        input_output_aliases={3: 0},
        scratch_shapes=[pltpu.SemaphoreType.DMA(())],
        compiler_params=_params("arbitrary"),
    )(pos1.reshape(M // tm, 1, tm), pos2.reshape(M // tm, 1, tm), hn, xs)


def _combine_body(i1_ref, i2_ref, x_ref, route_ref, ys_hbm, o_ref, b1, b2, sem1, sem2):
    tm = o_ref.shape[0]

    def start(r, _):
        _row_get(ys_hbm, i1_ref[0, 0, r], b1, r, sem1).start()
        _row_get(ys_hbm, i2_ref[0, 0, r], b2, r, sem2).start()
        return 0

    lax.fori_loop(0, tm, start, 0, unroll=8)

    def wait(r, _):
        _row_get(ys_hbm, 0, b1, r, sem1).wait()
        _row_get(ys_hbm, 0, b2, r, sem2).wait()
        return 0

    lax.fori_loop(0, tm, wait, 0, unroll=8)
    o_ref[...] = x_ref[...] + route_ref[:, 0:1] * b1[...] + route_ref[:, 1:2] * b2[...]


def combine_rows(x, route, ys, pos1, pos2, tm=256):
    M, D = x.shape
    tm = min(tm, M)
    idx_spec = pl.BlockSpec((1, 1, tm), lambda i: (i, 0, 0), memory_space=pltpu.SMEM)
    return pl.pallas_call(
        _combine_body,
        name="combine_rows",
        grid=(M // tm,),
        in_specs=[idx_spec, idx_spec, pl.BlockSpec((tm, D), lambda i: (i, 0)),
                  pl.BlockSpec((tm, 128), lambda i: (i, 0)), pl.BlockSpec(memory_space=pl.ANY)],
        out_specs=pl.BlockSpec((tm, D), lambda i: (i, 0)),
        out_shape=jax.ShapeDtypeStruct((M, D), F32),
        scratch_shapes=[pltpu.VMEM((tm, D), F32), pltpu.VMEM((tm, D), F32),
                        pltpu.SemaphoreType.DMA(()), pltpu.SemaphoreType.DMA(())],
        compiler_params=_params("arbitrary"),
    )(pos1.reshape(M // tm, 1, tm), pos2.reshape(M // tm, 1, tm), x, route, ys)


def moe_ffn(xs_list, norm_g, w_router, w1, w3, w2, tm_e):
    D = xs_list[0].shape[1]
    routed = [router(x, norm_g, w_router) for x in xs_list]
    top_i = jnp.concatenate([route[:, 2:4] for _, route in routed], axis=0).astype(jnp.int32)
    T = top_i.shape[0]
    e_flat = top_i.reshape(-1)
    onehot = (e_flat[:, None] == jnp.arange(N_EXPERTS, dtype=jnp.int32)[None, :]).astype(jnp.int32)
    csum = jnp.cumsum(onehot, axis=0)
    counts = csum[-1]
    rank = jnp.take_along_axis(csum, e_flat[:, None], axis=1)[:, 0] - 1
    ptiles = (counts + tm_e - 1) // tm_e
    tile_end = jnp.cumsum(ptiles)
    pos = (tile_end - ptiles)[e_flat] * tm_e + rank
    pos1, pos2 = pos[0::2], pos[1::2]
    n_tiles = (2 * T) // tm_e + N_EXPERTS
    tile_ids = jnp.arange(n_tiles, dtype=jnp.int32)
    tile_valid = (tile_ids < tile_end[-1]).astype(jnp.int32)
    last_e = jnp.max(jnp.where(counts > 0, jnp.arange(N_EXPERTS, dtype=jnp.int32), 0))
    tile_expert = jnp.sum((tile_end[None, :] <= tile_ids[:, None]).astype(jnp.int32), axis=1)
    tile_expert = jnp.minimum(tile_expert, last_e)
    xs = jnp.zeros((n_tiles * tm_e, D), F32)
    t0 = 0
    for hn, _ in routed:
        t1 = t0 + hn.shape[0]
        xs = dispatch_rows(hn, pos1[t0:t1], pos2[t0:t1], xs)
        t0 = t1
    ys = ffn_experts(xs, tile_expert, tile_valid, w1, w3, w2, tm_e)
    outs = []
    t0 = 0
    for x, (_, route) in zip(xs_list, routed):
        t1 = t0 + x.shape[0]
        outs.append(combine_rows(x, route, ys, pos1[t0:t1], pos2[t0:t1]))
        t0 = t1
    return outs


def _rope_lanes(v, cos, sin, lane):
    partner = jnp.where(lane < 80, pltpu.roll(v, 112, 1), pltpu.roll(v, 16, 1))
    return v * cos + partner * sin


def _mla_rows_body(x_ref, g_ref, wd_ref, qln_ref, kvln_ref, krn_ref, wuq_ref, qn_ref, cos_ref, sin_ref,
                   q_ref, ckv_ref, kr_ref):
    hn = _rms(x_ref[...], g_ref[...]).astype(BF16)
    d = jnp.dot(hn, wd_ref[...], preferred_element_type=F32)
    cq = _rms(d[:, 0:MLA_Q_LORA], qln_ref[...]).astype(BF16)
    ckv_ref[...] = _rms(d[:, MLA_Q_LORA:MLA_Q_LORA + MLA_KV_LORA], kvln_ref[...])
    cos = cos_ref[...]
    sin = sin_ref[...]
    lane = lax.broadcasted_iota(jnp.int32, cos.shape, 1)
    kr = d[:, MLA_Q_LORA + MLA_KV_LORA:]
    kr = kr * lax.rsqrt(jnp.sum(kr * kr, axis=-1, keepdims=True) * (1.0 / MLA_ROPE) + EPS) * krn_ref[...]
    kr_ref[...] = _rope_lanes(kr, cos, sin, lane)
    q = jnp.dot(cq, wuq_ref[...], preferred_element_type=F32)
    nope = lane < MLA_NOPE
    for h in range(MLA_HEADS):
        qh = q[:, h * 128:(h + 1) * 128]
        sq = qh * qh
        ss_n = jnp.sum(jnp.where(nope, sq, 0.0), axis=-1, keepdims=True) * (1.0 / MLA_NOPE)
        ss_r = jnp.sum(jnp.where(nope, 0.0, sq), axis=-1, keepdims=True) * (1.0 / MLA_ROPE)
        r = jnp.where(nope, lax.rsqrt(ss_n + EPS), lax.rsqrt(ss_r + EPS))
        qh = _rope_lanes(qh * r * qn_ref[...], cos, sin, lane)
        q_ref[:, h * 128:(h + 1) * 128] = (qh * (MLA_SCALE * math.log2(math.e))).astype(BF16)


def mla_rows(x, norm_g, wd_pad, q_lora_norm, kv_lora_norm, krn_pad, wuq_pad, qn_pad, cos_t, sin_t, tm=512):
    M, D = x.shape
    tm = min(tm, M)
    const = lambda shape: pl.BlockSpec(shape, lambda i: (0,) * len(shape))
    rows = lambda n: pl.BlockSpec((tm, n), lambda i: (i, 0))
    table_tiles = cos_t.shape[0] // tm
    table = pl.BlockSpec((tm, 128), lambda i: (i % table_tiles, 0))
    return pl.pallas_call(
        _mla_rows_body,
        name="mla_rows",
        grid=(M // tm,),
        in_specs=[rows(D), const((1, D)), const(wd_pad.shape), const((1, MLA_Q_LORA)), const((1, MLA_KV_LORA)),
                  const((1, 128)), const(wuq_pad.shape), const((1, 128)), table, table],
        out_specs=[rows(MLA_HEADS * 128), rows(MLA_KV_LORA), rows(128)],
        out_shape=[jax.ShapeDtypeStruct((M, MLA_HEADS * 128), BF16), jax.ShapeDtypeStruct((M, MLA_KV_LORA), F32),
                   jax.ShapeDtypeStruct((M, 128), F32)],
        compiler_params=_params("parallel"),
    )(x, norm_g.reshape(1, D), wd_pad, q_lora_norm.reshape(1, -1), kv_lora_norm.reshape(1, -1), krn_pad,
      wuq_pad, qn_pad, cos_t, sin_t)


def _mla_kv_body(ckv_ref, kr_ref, wuk_ref, wuv_ref, kn_ref, k_ref, v_ref):
    c = ckv_ref[...].astype(BF16)
    kn = jnp.dot(c, wuk_ref[...], preferred_element_type=F32)
    lane = lax.broadcasted_iota(jnp.int32, (1, MLA_HEADS * 128), 1)
    ones_col = jnp.where(jnp.bitwise_and(lane, 127) == MLA_V, 1.0, 0.0)
    v_ref[...] = (jnp.dot(c, wuv_ref[...], preferred_element_type=F32) + ones_col).astype(BF16)
    kr = kr_ref[...]
    for h in range(MLA_HEADS):
        kh = kn[:, h * 128:(h + 1) * 128]
        r = lax.rsqrt(jnp.sum(kh * kh, axis=-1, keepdims=True) * (1.0 / MLA_NOPE) + EPS)
        k_ref[:, h * 128:(h + 1) * 128] = (kh * r * kn_ref[...] + kr).astype(BF16)


def mla_kv(ckv, kr_pad, wuk_pad, wuv, kn_pad, tm=512):
    M = ckv.shape[0]
    tm = min(tm, M)
    const = lambda shape: pl.BlockSpec(shape, lambda i: (0,) * len(shape))
    rows = lambda n: pl.BlockSpec((tm, n), lambda i: (i, 0))
    return pl.pallas_call(
        _mla_kv_body,
        name="mla_kv",
        grid=(M // tm,),
        in_specs=[rows(MLA_KV_LORA), rows(128), const(wuk_pad.shape), const(wuv.shape), const((1, 128))],
        out_specs=[rows(MLA_HEADS * 128), rows(MLA_HEADS * 128)],
        out_shape=[jax.ShapeDtypeStruct((M, MLA_HEADS * 128), BF16),
                   jax.ShapeDtypeStruct((M, MLA_HEADS * 128), BF16)],
        compiler_params=_params("parallel"),
    )(ckv, kr_pad, wuk_pad, wuv, kn_pad)


def _flash_body(qi_ref, ki_ref, q_ref, k_ref, v_ref, *rest, causal):
    if causal:
        bias_ref, o_ref, m_ref, acc_ref = rest
    else:
        o_ref, m_ref, acc_ref = rest
    t = pl.program_id(2)
    qi = qi_ref[t]
    ki = ki_ref[t]
    tq = q_ref.shape[1]

    @pl.when(ki == 0)
    def _():
        m_ref[...] = jnp.full_like(m_ref, -jnp.inf)
        acc_ref[...] = jnp.zeros_like(acc_ref)

    def step(masked):
        tk = k_ref.shape[1]
        sub = min(FLASH_KV_SUB, tk) if masked else tk
        for h in range(2):
            hl = slice(h * 128, (h + 1) * 128)
            m, acc = m_ref[h], acc_ref[h]
            for j in range(tk // sub):
                r0 = j * sub if masked else 0
                cols = slice(j * sub, (j + 1) * sub)
                s = lax.dot_general(q_ref[0, r0:, hl], k_ref[0, cols, hl], NT_DIMS, preferred_element_type=F32)
                if masked:
                    s = s + bias_ref[r0:, cols]
                m_new = jnp.maximum(m[r0:], jnp.max(s, axis=-1, keepdims=True))
                alpha = jnp.exp2(m[r0:] - m_new)
                p = jnp.exp2(s - m_new)
                acc_new = alpha * acc[r0:] + jnp.dot(p.astype(BF16), v_ref[0, cols, hl], preferred_element_type=F32)
                if r0:
                    m_new = jnp.concatenate([m[:r0], m_new], axis=0)
                    acc_new = jnp.concatenate([acc[:r0], acc_new], axis=0)
                m, acc = m_new, acc_new
            m_ref[h], acc_ref[h] = m, acc

    def finish():
        lane = lax.broadcasted_iota(jnp.int32, (tq, 128), 1)
        o0 = acc_ref[0] / acc_ref[0, :, MLA_V:MLA_V + 1]
        o1 = acc_ref[1] / acc_ref[1, :, MLA_V:MLA_V + 1]
        o_ref[0] = jnp.where(lane < MLA_V, o0, pltpu.roll(o1, MLA_V, 1)).astype(o_ref.dtype)

    if causal:
        @pl.when(ki < qi)
        def _():
            step(False)

        @pl.when(ki == qi)
        def _():
            step(True)
            finish()
    else:
        step(False)

        @pl.when(t == pl.num_programs(2) - 1)
        def _():
            finish()


def mla_attention(q, k, v, causal, tq, tk):
    B, Lq, _ = q.shape
    Lk = k.shape[1]
    nq, nk = Lq // tq, Lk // tk
    pairs = [(i, j) for i in range(nq) for j in range(nk) if (j <= i or not causal)]
    qi_tab = jnp.asarray([i for i, _ in pairs], jnp.int32)
    ki_tab = jnp.asarray([j for _, j in pairs], jnp.int32)
    in_specs = [pl.BlockSpec((1, tq, 256), lambda b, p, t, qt, kt: (b, qt[t], p)),
                pl.BlockSpec((1, tk, 256), lambda b, p, t, qt, kt: (b, kt[t], p)),
                pl.BlockSpec((1, tk, 256), lambda b, p, t, qt, kt: (b, kt[t], p))]
    args = [q, k, v]
    if causal:
        chunk = jnp.arange(tq) // CHUNK
        bias = jnp.where(chunk[None, :] <= chunk[:, None], 0.0, -jnp.inf).astype(F32)
        in_specs.append(pl.BlockSpec((tq, tk), lambda b, p, t, qt, kt: (0, 0)))
        args.append(bias)
    grid_spec = pltpu.PrefetchScalarGridSpec(
        num_scalar_prefetch=2,
        grid=(B, MLA_HEADS // 2, len(pairs)),
        in_specs=in_specs,
        out_specs=pl.BlockSpec((1, tq, 128), lambda b, p, t, qt, kt: (b, qt[t], p)),
        scratch_shapes=[pltpu.VMEM((2, tq, 1), F32), pltpu.VMEM((2, tq, 128), F32)],
    )
    return pl.pallas_call(
        functools.partial(_flash_body, causal=causal),
        name="mla_flash",
        grid_spec=grid_spec,
        out_shape=jax.ShapeDtypeStruct((B, Lq, MLA_HEADS * MLA_V), BF16),
        compiler_params=_params("parallel", "parallel", "arbitrary"),
    )(qi_tab, ki_tab, *args)


def _rope_tables(pos):
    half = MLA_ROPE // 2
    inv = ROPE_THETA ** (-jnp.arange(half, dtype=F32) / half)
    ang = pos.astype(F32)[:, None] * inv[None, :]
    cos, sin = jnp.cos(ang), jnp.sin(ang)
    n = pos.shape[0]
    ones = jnp.ones((n, MLA_NOPE), F32)
    zeros = jnp.zeros((n, 128 - MLA_NOPE - MLA_ROPE), F32)
    cos_t = jnp.concatenate([ones, cos, cos, zeros], axis=1)
    sin_t = jnp.concatenate([jnp.zeros((n, MLA_NOPE), F32), -sin, sin, zeros], axis=1)
    return cos_t, sin_t


def _pad_heads(w, d_in, d_head, d_pad):
    w = w.reshape(d_in, MLA_HEADS, d_head)
    return jnp.pad(w, ((0, 0), (0, 0), (0, d_pad - d_head))).reshape(d_in, MLA_HEADS * d_pad)


def mla_weights(norm_g, w_down, q_lora_norm, kv_lora_norm, w_uq, w_ukv, q_nope_norm, q_rope_norm, k_nope_norm,
                k_rope_norm, w_o):
    D = w_down.shape[0]
    split = MLA_Q_LORA + MLA_KV_LORA
    tail = 128 - MLA_NOPE - MLA_ROPE
    wd_pad = jnp.concatenate([w_down[:, :split], jnp.zeros((D, MLA_NOPE), F32), w_down[:, split:],
                              jnp.zeros((D, tail), F32)], axis=1).astype(BF16)
    lane_pad = lambda a, b: jnp.concatenate([a, b, jnp.zeros((128 - a.shape[0] - b.shape[0],), F32)]).reshape(1, 128)
    krn_pad = lane_pad(jnp.zeros((MLA_NOPE,), F32), k_rope_norm)
    qn_pad = lane_pad(q_nope_norm, q_rope_norm)
    kn_pad = lane_pad(k_nope_norm, jnp.zeros((0,), F32))
    wuq_pad = _pad_heads(w_uq, MLA_Q_LORA, MLA_NOPE + MLA_ROPE, 128).astype(BF16)
    wukv = w_ukv.reshape(MLA_KV_LORA, MLA_HEADS, MLA_NOPE + MLA_V)
    wuk_pad = _pad_heads(wukv[:, :, :MLA_NOPE].reshape(MLA_KV_LORA, -1), MLA_KV_LORA, MLA_NOPE, 128).astype(BF16)
    wuv = _pad_heads(wukv[:, :, MLA_NOPE:].reshape(MLA_KV_LORA, -1), MLA_KV_LORA, MLA_V, 128).astype(BF16)
    rows_w = (norm_g, wd_pad, q_lora_norm, kv_lora_norm, krn_pad, wuq_pad, qn_pad)
    return rows_w, (wuk_pad, wuv, kn_pad), w_o.astype(BF16)


def mla_block(x, b, l, mw, ckv_past, kr_past, tile):
    rows_w, kv_w, w_o = mw
    n_past = 0 if ckv_past is None else ckv_past.shape[1]
    cos_t, sin_t = _rope_tables(n_past + jnp.arange(l))
    reps = max(1, min(b * l, 512) // l)
    q, ckv, kr = mla_rows(x, *rows_w, jnp.tile(cos_t, (reps, 1)), jnp.tile(sin_t, (reps, 1)))
    kr_out = kr[:, MLA_NOPE:MLA_NOPE + MLA_ROPE]
    if ckv_past is None:
        k_all, v_all = mla_kv(ckv, kr, *kv_w)
        o = mla_attention(q.reshape(b, l, -1), k_all.reshape(b, l, -1), v_all.reshape(b, l, -1), True, tile, tile)
    else:
        lk = n_past + l
        ckv_all = jnp.concatenate([ckv_past, ckv.reshape(b, l, -1)], axis=1).reshape(b * lk, -1)
        kr_pad = jnp.pad(kr_past, ((0, 0), (0, 0), (MLA_NOPE, 128 - MLA_NOPE - MLA_ROPE)))
        kr_all = jnp.concatenate([kr_pad, kr.reshape(b, l, 128)], axis=1).reshape(b * lk, 128)
        k_all, v_all = mla_kv(ckv_all, kr_all, *kv_w, tm=lk // 5)
        o = mla_attention(q.reshape(b, l, -1), k_all.reshape(b, lk, -1), v_all.reshape(b, lk, -1), False, l, lk)
    x = rms_matmul(o.reshape(b * l, -1), w_o, res=x, tm=1024, name="mla_out_proj")
    return x, ckv, kr_out


def kernel(x_prompt, x_sample, mem_prompt, state_conv_a, state_conv_ssm, state_ssm, cache_mla_ckv, cache_mla_krope, cache_mem_k, cache_mem_v, norm_mix, norm_mem, norm_ffn, w_in_e, conv_a_w, conv_s_w, conv_s_b, dt_bias, a_log, d_skip, ssm_norm, w_out_e, w_down_o, q_lora_norm, kv_lora_norm, w_uq, w_ukv, q_nope_norm, q_rope_norm, k_nope_norm, k_rope_norm, w_o_mla, mem_src_norm, w_mem_q, w_mem_k, w_mem_v, mem_q_norm, mem_k_norm, w_mem_o, w_ffn1, w_ffn3, w_ffn2, w_router, w_exp1, w_exp3, w_exp2):
    bp, lp, D = x_prompt.shape
    bs, ls, _ = x_sample.shape
    past = cache_mla_ckv.shape[2]
    bf = lambda w: w.astype(BF16)
    xp = x_prompt.reshape(bp * lp, D)
    xs = x_sample.reshape(bs * ls, D)

    ew = even_weights(w_in_e[0], w_out_e[0], norm_mix[0], conv_a_w[0], conv_s_w[0], conv_s_b[0], dt_bias[0],
                      a_log[0], d_skip[0], ssm_norm[0])
    xp, p_ca, p_cs, p_st = even_mixer_block(
        xp, bp, lp, ew, jnp.zeros((bp, 2, A_WIDTH), F32), jnp.zeros((bp, 3, SSM_XBC), F32),
        jnp.zeros((bp, SSM_HEADS, SSM_HEAD_DIM, SSM_STATE), F32), 1024)
    xs, s_ca, s_cs, s_st = even_mixer_block(xs, bs, ls, ew, state_conv_a[0], state_conv_ssm[0], state_ssm[0], 512)

    p_mk, p_mv = [], []

    def memory_block(i, xp, xs):
        mk, mv, mk_b, mv_b = mem_kv(mem_prompt, mem_src_norm[i], bf(w_mem_k[i]), bf(w_mem_v[i]), mem_k_norm[i])
        p_mk.append(mk.reshape(bp, MEM_TOKENS, MEM_HEADS, MEM_HEAD_DIM))
        p_mv.append(mv.reshape(bp, MEM_TOKENS, MEM_HEADS, MEM_HEAD_DIM))
        wq, wo = bf(w_mem_q[i]), bf(w_mem_o[i])
        xp = mem_attend(xp, norm_mem[i], mk_b, mv_b, wq, mem_q_norm[i], wo, lp)
        ck = bf(cache_mem_k[i]).reshape(bs, MEM_TOKENS, D)
        cv = bf(cache_mem_v[i]).reshape(bs, MEM_TOKENS, D)
        xs = mem_attend(xs, norm_mem[i], ck, cv, wq, mem_q_norm[i], wo, ls)
        return xp, xs

    xp, xs = memory_block(0, xp, xs)
    w1, w3, w2 = bf(w_ffn1[0]), bf(w_ffn3[0]), bf(w_ffn2[0])
    xp = ffn_dense(xp, norm_ffn[0], w1, w3, w2)
    xs = ffn_dense(xs, norm_ffn[0], w1, w3, w2)

    mw = mla_weights(norm_mix[1], w_down_o[0], q_lora_norm[0], kv_lora_norm[0], w_uq[0], w_ukv[0], q_nope_norm[0],
                     q_rope_norm[0], k_nope_norm[0], k_rope_norm[0], w_o_mla[0])
    xp, p_ckv, p_kr = mla_block(xp, bp, lp, mw, None, None, min(lp, 1024))
    xs, s_ckv, s_kr = mla_block(xs, bs, ls, mw, cache_mla_ckv[0], cache_mla_krope[0], None)

    xp, xs = memory_block(1, xp, xs)
    we1, we3, we2 = bf(w_exp1[0]), bf(w_exp3[0]), bf(w_exp2[0])
    xp, xs = moe_ffn([xp, xs], norm_ffn[1], w_router[0], we1, we3, we2, 512)

    return (xp.reshape(bp, lp, D), xs.reshape(bs, ls, D),
            p_ca[None], p_cs[None], p_st[None],
            p_ckv.reshape(1, bp, lp, -1), p_kr.reshape(1, bp, lp, -1), jnp.stack(p_mk), jnp.stack(p_mv),
            s_ca[None], s_cs[None], s_st[None],
            s_ckv.reshape(1, bs, ls, -1), s_kr.reshape(1, bs, ls, -1))
```

```python
import functools
import math

import jax
import jax.numpy as jnp
from jax import lax
from jax.experimental import pallas as pl
from jax.experimental.pallas import tpu as pltpu

F32 = jnp.float32
BF16 = jnp.bfloat16
EPS = 1e-6

D_MODEL = 1024
CHUNK = 64
A_WIDTH = 1024
SSM_HEADS = 32
SSM_HEAD_DIM = 64
SSM_INNER = 2048
SSM_STATE = 128
SSM_GROUPS = 4
SSM_XBC = 3072
HEAD_LANES = 128
PROJ_W = 3 * A_WIDTH + SSM_INNER + SSM_XBC + HEAD_LANES
COL_Z = 3 * A_WIDTH
COL_XBC = COL_Z + SSM_INNER
COL_DT = COL_XBC + SSM_XBC
MLA_HEADS = 16
MLA_Q_LORA = 512
MLA_KV_LORA = 256
MLA_NOPE = 64
MLA_ROPE = 32
MLA_V = 64
MLA_SCALE = (MLA_NOPE + MLA_ROPE) ** -0.5
ROPE_THETA = 10000.0
MEM_TOKENS = 256
MEM_HEADS = 4
MEM_HEAD_DIM = 256
N_EXPERTS = 8
VMEM_LIMIT = 56 * 1024 * 1024
MIXER_CHUNKS_PER_STEP = 2
FLASH_KV_SUB = 256
NT_DIMS = (((1,), (1,)), ((), ()))


def _params(*sem):
    return pltpu.CompilerParams(dimension_semantics=sem, vmem_limit_bytes=VMEM_LIMIT)


def _rms(x, g):
    return x * lax.rsqrt(jnp.mean(x * x, axis=-1, keepdims=True) + EPS) * g


def _split3(x):
    hi = x.astype(BF16)
    r1 = x - hi.astype(F32)
    mid = r1.astype(BF16)
    lo = (r1 - mid.astype(F32)).astype(BF16)
    return hi, mid, lo


def _dot_sel(x, sel_bf16):
    hi, mid, lo = _split3(x)
    d = lambda a: jnp.dot(a, sel_bf16, preferred_element_type=F32)
    return d(hi) + d(mid) + d(lo)


def _dot_sel2(x, sel_bf16):
    hi = x.astype(BF16)
    mid = (x - hi.astype(F32)).astype(BF16)
    d = lambda a: jnp.dot(a, sel_bf16, preferred_element_type=F32)
    return d(hi) + d(mid)


def _sel_dot_nt(sel_bf16, x):
    hi, mid, lo = _split3(x)
    d = lambda a: lax.dot_general(sel_bf16, a, NT_DIMS, preferred_element_type=F32)
    return d(hi) + d(mid) + d(lo)


def _sel_dot(sel_bf16, x):
    hi, mid, lo = _split3(x)
    d = lambda a: jnp.dot(sel_bf16, a, preferred_element_type=F32)
    return d(hi) + d(mid) + d(lo)


def _rms_matmul_body(*refs, use_norm, use_res):
    it = iter(refs)
    x_ref = next(it)
    g_ref = next(it) if use_norm else None
    w_ref = next(it)
    r_ref = next(it) if use_res else None
    o_ref = next(it)
    xn_ref = next(it) if use_norm else None
    if use_norm:
        @pl.when(pl.program_id(1) == 0)
        def _():
            xn_ref[...] = _rms(x_ref[...], g_ref[...]).astype(BF16)
        xb = xn_ref[...]
    else:
        xb = x_ref[...]
    acc = jnp.dot(xb, w_ref[...], preferred_element_type=F32)
    if use_res:
        acc = acc + r_ref[...]
    o_ref[...] = acc.astype(o_ref.dtype)


def rms_matmul(x, w, g=None, res=None, out_dtype=F32, tm=512, tn=None, name="rms_matmul"):
    M, K = x.shape
    N = w.shape[1]
    tn = tn or N
    tm = min(tm, M)
    in_specs = [pl.BlockSpec((tm, K), lambda i, j: (i, 0))]
    args = [x]
    if g is not None:
        in_specs.append(pl.BlockSpec((1, K), lambda i, j: (0, 0)))
        args.append(g.reshape(1, K))
    in_specs.append(pl.BlockSpec((K, tn), lambda i, j: (0, j)))
    args.append(w)
    if res is not None:
        in_specs.append(pl.BlockSpec((tm, tn), lambda i, j: (i, j)))
        args.append(res)
    scratch = [pltpu.VMEM((tm, K), BF16)] if g is not None else []
    return pl.pallas_call(
        functools.partial(_rms_matmul_body, use_norm=g is not None, use_res=res is not None),
        name=name,
        grid=(M // tm, N // tn),
        in_specs=in_specs,
        out_specs=pl.BlockSpec((tm, tn), lambda i, j: (i, j)),
        out_shape=jax.ShapeDtypeStruct((M, N), out_dtype),
        scratch_shapes=scratch,
        compiler_params=_params("parallel", "arbitrary"),
    )(*args)


def _softplus(v):
    return jnp.maximum(v, 0.0) + jnp.log1p(jnp.exp(-jnp.abs(v)))


def _mixer_body(proj_ref, wa_ref, ws_ref, bs_ref, dtb_ref, alog_ref, dskip_ref, gn_ref,
                tri_ref, selrow_ref, expand_ref, eye_ref,
                ca_prev_ref, cs_prev_ref, st_prev_ref,
                mix_ref, ca_out_ref, cs_out_ref, st_ref,
                abuf, sbuf):
    c = pl.program_id(1)
    Q = CHUNK
    R = proj_ref.shape[1]

    @pl.when(c == 0)
    def _():
        abuf[0:8, :] = ca_prev_ref[0]
        sbuf[0:8, :] = cs_prev_ref[0]
        st_ref[0] = st_prev_ref[0]

    g_b = proj_ref[0, :, 0:A_WIDTH]
    abuf[8:8 + R, :] = proj_ref[0, :, A_WIDTH:2 * A_WIDTH] * proj_ref[0, :, 2 * A_WIDTH:3 * A_WIDTH]
    conv_a = wa_ref[0:1, :] * abuf[6:6 + R, :]
    conv_a = conv_a + wa_ref[1:2, :] * abuf[7:7 + R, :]
    conv_a = conv_a + wa_ref[2:3, :] * abuf[8:8 + R, :]
    mix_ref[0, :, 0:A_WIDTH] = (g_b * conv_a).astype(BF16)
    tail_a = abuf[R:R + 8, :]
    ca_out_ref[0] = tail_a
    abuf[0:8, :] = tail_a

    sbuf[8:8 + R, :] = proj_ref[0, :, COL_XBC:COL_DT]
    xc_all = ws_ref[0:1, :] * sbuf[5:5 + R, :]
    xc_all = xc_all + ws_ref[1:2, :] * sbuf[6:6 + R, :]
    xc_all = xc_all + ws_ref[2:3, :] * sbuf[7:7 + R, :]
    xc_all = xc_all + ws_ref[3:4, :] * sbuf[8:8 + R, :]
    xc_all = xc_all + bs_ref[...]
    xc_all = xc_all * jax.nn.sigmoid(xc_all)
    tail_s = sbuf[R:R + 8, :]
    cs_out_ref[0] = tail_s
    sbuf[0:8, :] = tail_s

    dt_all = _softplus(proj_ref[0, :, COL_DT:COL_DT + HEAD_LANES] + dtb_ref[...])
    la_all = dt_all * (-jnp.exp(alog_ref[...]))

    lane = lax.broadcasted_iota(jnp.int32, (Q, 128), 1)
    row = lax.broadcasted_iota(jnp.int32, (Q, 128), 0)
    tril2 = jnp.where(lane >= Q, lane - Q, lane) <= row
    left = lane < Q
    eye = eye_ref[...]

    for ci in range(R // Q):
        _ssd_chunk(slice(ci * Q, (ci + 1) * Q), xc_all, dt_all, la_all, tril2, left, eye,
                   proj_ref, dskip_ref, gn_ref, tri_ref, selrow_ref, expand_ref, mix_ref, st_ref)


def _ssd_chunk(rows, xc_all, dt_all, la_all, tril2, left, eye,
               proj_ref, dskip_ref, gn_ref, tri_ref, selrow_ref, expand_ref, mix_ref, st_ref):
    Q = CHUNK
    xc = xc_all[rows]
    dt = dt_all[rows]
    cs = _sel_dot(tri_ref[...], la_all[rows])
    cs_last = cs[Q - 1:Q, :]
    w_st = dt * jnp.exp(cs_last - cs)
    cs_x = _dot_sel(cs, expand_ref[...])
    w_st_x = _dot_sel2(w_st, expand_ref[...])
    e_in_x = jnp.exp(cs_x)
    dec_x = e_in_x[Q - 1:Q, :]
    cs_r = _sel_dot_nt(selrow_ref[...], cs)
    dt_r = _sel_dot_nt(selrow_ref[...], dt)
    cs_rp = jnp.concatenate([cs_r[0:16], cs_r[16:32]], axis=1)
    dt_rp = jnp.concatenate([dt_r[0:16], dt_r[16:32]], axis=1)

    for g in range(SSM_GROUPS):
        gl = slice(SSM_INNER + g * SSM_STATE, SSM_INNER + (g + 1) * SSM_STATE)
        Bg = xc[:, gl].astype(BF16)
        Cg = xc[:, gl.start + SSM_GROUPS * SSM_STATE: gl.stop + SSM_GROUPS * SSM_STATE].astype(BF16)
        sc = lax.dot_general(Cg, Bg, NT_DIMS, preferred_element_type=F32)
        sc2 = jnp.concatenate([sc, sc], axis=1)
        BgT = lax.dot_general(eye, Bg, NT_DIMS, preferred_element_type=F32).astype(BF16)
        hs = slice(g * 512, (g + 1) * 512)
        st_g = st_ref[0, :, hs]
        y_off = jnp.dot(Cg, st_g.astype(BF16), preferred_element_type=F32)
        xw = (xc[:, hs] * w_st_x[:, hs]).astype(BF16)
        st_ref[0, :, hs] = dec_x[:, hs] * st_g + jnp.dot(BgT, xw, preferred_element_type=F32)
        ys = []
        for kk in range(4):
            k = 4 * g + kk
            pl_ = slice(k * 128, (k + 1) * 128)
            diff = cs_x[:, pl_] - cs_rp[k:k + 1, :]
            decay = jnp.exp(jnp.where(tril2, diff, -jnp.inf))
            m_pair = (sc2 * decay * dt_rp[k:k + 1, :]).astype(BF16)
            xp = xc[:, pl_]
            rhs = jnp.concatenate([jnp.where(left, xp, 0.0), jnp.where(left, 0.0, xp)], axis=0).astype(BF16)
            y_diag = jnp.dot(m_pair, rhs, preferred_element_type=F32)
            y = y_diag + y_off[:, kk * 128:(kk + 1) * 128] * e_in_x[:, pl_]
            y = y + dskip_ref[:, pl_] * xp
            z = proj_ref[0, rows, COL_Z + k * 128: COL_Z + (k + 1) * 128]
            ys.append(y * (z * jax.nn.sigmoid(z)))
        yg = jnp.concatenate(ys, axis=1)
        yn = yg * lax.rsqrt(jnp.mean(yg * yg, axis=-1, keepdims=True) + EPS) * gn_ref[:, hs]
        mix_ref[0, rows, A_WIDTH + g * 512: A_WIDTH + (g + 1) * 512] = yn.astype(BF16)


def mixer_even(proj, conv_a_w, conv_s_w, conv_s_b, dt_bias, a_log, d_skip, ssm_norm,
               conv_a_prev8, conv_s_prev8, ssm_prev_t):
    B, L, _ = proj.shape
    rows = CHUNK * MIXER_CHUNKS_PER_STEP if L % (CHUNK * MIXER_CHUNKS_PER_STEP) == 0 else CHUNK
    pad_h =lambda v: jnp.pad(v.reshape(1, SSM_HEADS), ((0, 0), (0, HEAD_LANES - SSM_HEADS)))
    tri = (jnp.arange(CHUNK)[:, None] >= jnp.arange(CHUNK)[None, :]).astype(BF16)
    heads = jnp.arange(HEAD_LANES)
    order = jnp.concatenate([jnp.arange(0, SSM_HEADS, 2), jnp.arange(1, SSM_HEADS, 2)])
    selrow = (order[:, None] == heads[None, :]).astype(BF16)
    expand = (heads[:, None] == (jnp.arange(SSM_INNER) // SSM_HEAD_DIM)[None, :]).astype(BF16)
    eye = jnp.eye(SSM_STATE, dtype=BF16)
    dskip_x = jnp.repeat(d_skip.astype(F32), SSM_HEAD_DIM).reshape(1, SSM_INNER)
    const = lambda shape: pl.BlockSpec(shape, lambda b, c: (0,) * len(shape))
    per_b = lambda shape: pl.BlockSpec((1,) + shape, lambda b, c: (b,) + (0,) * len(shape))
    return pl.pallas_call(
        _mixer_body,
        name="ssd_mixer",
        grid=(B, L // rows),
        in_specs=[
            pl.BlockSpec((1, rows, PROJ_W), lambda b, c: (b, c, 0)),
            const((3, A_WIDTH)), const((4, SSM_XBC)), const((1, SSM_XBC)),
            const((1, HEAD_LANES)), const((1, HEAD_LANES)), const((1, SSM_INNER)), const((1, SSM_INNER)),
            const((CHUNK, CHUNK)), const((SSM_HEADS, HEAD_LANES)), const((HEAD_LANES, SSM_INNER)),
            const((SSM_STATE, SSM_STATE)),
            per_b((8, A_WIDTH)), per_b((8, SSM_XBC)), per_b((SSM_STATE, SSM_INNER)),
        ],
        out_specs=[
            pl.BlockSpec((1, rows, A_WIDTH + SSM_INNER), lambda b, c: (b, c, 0)),
            per_b((8, A_WIDTH)), per_b((8, SSM_XBC)), per_b((SSM_STATE, SSM_INNER)),
        ],
        out_shape=[
            jax.ShapeDtypeStruct((B, L, A_WIDTH + SSM_INNER), BF16),
            jax.ShapeDtypeStruct((B, 8, A_WIDTH), F32),
            jax.ShapeDtypeStruct((B, 8, SSM_XBC), F32),
            jax.ShapeDtypeStruct((B, SSM_STATE, SSM_INNER), F32),
        ],
        scratch_shapes=[
            pltpu.VMEM((rows + 8, A_WIDTH), F32),
            pltpu.VMEM((rows + 8, SSM_XBC), F32),
        ],
        compiler_params=_params("parallel", "arbitrary"),
    )(proj, conv_a_w, conv_s_w, conv_s_b.reshape(1, SSM_XBC), pad_h(dt_bias), pad_h(a_log), dskip_x,
      ssm_norm.reshape(1, SSM_INNER), tri, selrow, expand, eye, conv_a_prev8, conv_s_prev8, ssm_prev_t)


def even_weights(w_in, w_out, norm_g, conv_a_w, conv_s_w, conv_s_b, dt_bias, a_log, d_skip, ssm_norm):
    w_in_pad = jnp.pad(w_in, ((0, 0), (0, PROJ_W - w_in.shape[1]))).astype(BF16)
    return (norm_g, w_in_pad, w_out.astype(BF16), (conv_a_w, conv_s_w, conv_s_b, dt_bias, a_log, d_skip, ssm_norm))


def even_mixer_block(x, b, l, ew, ca_prev, cs_prev, st_prev, tm):
    norm_g, w_in_pad, w_out, mixer_w = ew
    proj = rms_matmul(x, w_in_pad, g=norm_g, tm=tm, tn=PROJ_W // 5, name="mixer_in_proj")
    ca8 = jnp.pad(ca_prev, ((0, 0), (6, 0), (0, 0)))
    cs8 = jnp.pad(cs_prev, ((0, 0), (5, 0), (0, 0)))
    st_t = jnp.transpose(st_prev, (0, 3, 1, 2)).reshape(b, SSM_STATE, SSM_INNER)
    mix, ca, cs, st = mixer_even(proj.reshape(b, l, PROJ_W), *mixer_w, ca8, cs8, st_t)
    x = rms_matmul(mix.reshape(b * l, -1), w_out, res=x, tm=tm, name="mixer_out_proj")
    st = jnp.transpose(st.reshape(b, SSM_STATE, SSM_HEADS, SSM_HEAD_DIM), (0, 2, 3, 1))
    return x, ca[:, 6:8], cs[:, 5:8], st


def _mem_kv_body(mem_ref, g_ref, wk_ref, wv_ref, kn_ref, k_ref, v_ref, kb_ref, vb_ref):
    mm = _rms(mem_ref[0], g_ref[...]).astype(BF16)
    k = jnp.dot(mm, wk_ref[...], preferred_element_type=F32)
    v = jnp.dot(mm, wv_ref[...], preferred_element_type=F32)
    for h in range(MEM_HEADS):
        sl = slice(h * MEM_HEAD_DIM, (h + 1) * MEM_HEAD_DIM)
        kh = _rms(k[:, sl], kn_ref[...])
        k_ref[0, :, sl] = kh
        kb_ref[0, :, sl] = kh.astype(BF16)
    v_ref[0] = v
    vb_ref[0] = v.astype(BF16)


def mem_kv(mem, src_norm, w_k, w_v, k_norm):
    B, Mt, D = mem.shape
    const = lambda shape: pl.BlockSpec(shape, lambda b: (0,) * len(shape))
    blk = pl.BlockSpec((1, Mt, D), lambda b: (b, 0, 0))
    return pl.pallas_call(
        _mem_kv_body,
        name="mem_kv",
        grid=(B,),
        in_specs=[blk, const((1, D)), const((D, D)), const((D, D)), const((1, MEM_HEAD_DIM))],
        out_specs=[blk, blk, blk, blk],
        out_shape=[jax.ShapeDtypeStruct((B, Mt, D), F32), jax.ShapeDtypeStruct((B, Mt, D), F32),
                   jax.ShapeDtypeStruct((B, Mt, D), BF16), jax.ShapeDtypeStruct((B, Mt, D), BF16)],
        compiler_params=_params("parallel"),
    )(mem, src_norm.reshape(1, D), w_k, w_v, k_norm.reshape(1, MEM_HEAD_DIM))


def _mem_attn_body(x_ref, g_ref, wq_ref, qn_ref, k_ref, v_ref, wo_ref, o_ref, *, nb):
    x = x_ref[...]
    tm = x.shape[0]
    rows = tm // nb
    hn = _rms(x, g_ref[...]).astype(BF16)
    q = jnp.dot(hn, wq_ref[...], preferred_element_type=F32)
    scale = MEM_HEAD_DIM ** -0.5
    outs = []
    for h in range(MEM_HEADS):
        sl = slice(h * MEM_HEAD_DIM, (h + 1) * MEM_HEAD_DIM)
        qh = _rms(q[:, sl], qn_ref[...]).astype(BF16)
        parts = []
        for s in range(nb):
            qs = qh[s * rows:(s + 1) * rows]
            sc = lax.dot_general(qs, k_ref[s, :, sl], NT_DIMS, preferred_element_type=F32) * scale
            sc = sc - jnp.max(sc, axis=-1, keepdims=True)
            p = jnp.exp(sc)
            p = p / jnp.sum(p, axis=-1, keepdims=True)
            parts.append(jnp.dot(p.astype(BF16), v_ref[s, :, sl], preferred_element_type=F32))
        outs.append(parts[0] if nb == 1 else jnp.concatenate(parts, axis=0))
    o = jnp.concatenate(outs, axis=1).astype(BF16)
    o_ref[...] = x + jnp.dot(o, wo_ref[...], preferred_element_type=F32)


def mem_attend(x, norm_g, k_b, v_b, w_q, q_norm, w_o, rows_per_seq, tm=512):
    M, D = x.shape
    tm = min(tm, M)
    nb = max(1, tm // rows_per_seq)
    tiles_per_seq = max(1, rows_per_seq // tm)
    const = lambda shape: pl.BlockSpec(shape, lambda i: (0,) * len(shape))
    kv_spec = pl.BlockSpec((nb, MEM_TOKENS, D), lambda i: (i // tiles_per_seq, 0, 0))
    return pl.pallas_call(
        functools.partial(_mem_attn_body, nb=nb),
        name="mem_attn",
        grid=(M // tm,),
        in_specs=[pl.BlockSpec((tm, D), lambda i: (i, 0)), const((1, D)), const((D, D)),
                  const((1, MEM_HEAD_DIM)), kv_spec, kv_spec, const((D, D))],
        out_specs=pl.BlockSpec((tm, D), lambda i: (i, 0)),
        out_shape=jax.ShapeDtypeStruct((M, D), F32),
        compiler_params=_params("parallel"),
    )(x, norm_g.reshape(1, D), w_q, q_norm.reshape(1, MEM_HEAD_DIM), k_b, v_b, w_o)


def _swiglu_step(xn, w1, w3, w2):
    h1 = jnp.dot(xn, w1, preferred_element_type=F32)
    h3 = jnp.dot(xn, w3, preferred_element_type=F32)
    a = (h1 * jax.nn.sigmoid(h1) * h3).astype(BF16)
    return jnp.dot(a, w2, preferred_element_type=F32)


def _ffn_dense_body(x_ref, g_ref, w1_ref, w3_ref, w2_ref, o_ref, xn_ref, acc_ref):
    f = pl.program_id(1)

    @pl.when(f == 0)
    def _():
        xn_ref[...] = _rms(x_ref[...], g_ref[...]).astype(BF16)
        acc_ref[...] = jnp.zeros_like(acc_ref)

    acc_ref[...] += _swiglu_step(xn_ref[...], w1_ref[...], w3_ref[...], w2_ref[...])

    @pl.when(f == pl.num_programs(1) - 1)
    def _():
        o_ref[...] = x_ref[...] + acc_ref[...]


def ffn_dense(x, norm_g, w1, w3, w2, tm=512, tf=1408):
    M, D = x.shape
    F = w1.shape[1]
    tm = min(tm, M)
    return pl.pallas_call(
        _ffn_dense_body,
        name="ffn_dense",
        grid=(M // tm, F // tf),
        in_specs=[pl.BlockSpec((tm, D), lambda i, f: (i, 0)), pl.BlockSpec((1, D), lambda i, f: (0, 0)),
                  pl.BlockSpec((D, tf), lambda i, f: (0, f)), pl.BlockSpec((D, tf), lambda i, f: (0, f)),
                  pl.BlockSpec((tf, D), lambda i, f: (f, 0))],
        out_specs=pl.BlockSpec((tm, D), lambda i, f: (i, 0)),
        out_shape=jax.ShapeDtypeStruct((M, D), F32),
        scratch_shapes=[pltpu.VMEM((tm, D), BF16), pltpu.VMEM((tm, D), F32)],
        compiler_params=_params("parallel", "arbitrary"),
    )(x, norm_g.reshape(1, D), w1, w3, w2)


def _ffn_expert_body(te_ref, tv_ref, x_ref, w1_ref, w3_ref, w2_ref, o_ref, xn_ref, acc_ref):
    i = pl.program_id(0)
    f = pl.program_id(1)
    last = f == pl.num_programs(1) - 1

    @pl.when(tv_ref[i] > 0)
    def _():
        @pl.when(f == 0)
        def _():
            xn_ref[...] = x_ref[...].astype(BF16)
            acc_ref[...] = jnp.zeros_like(acc_ref)

        acc_ref[...] += _swiglu_step(xn_ref[...], w1_ref[0], w3_ref[0], w2_ref[0])

        @pl.when(last)
        def _():
            o_ref[...] = acc_ref[...]

    @pl.when(jnp.logical_and(tv_ref[i] == 0, last))
    def _():
        o_ref[...] = jnp.zeros_like(o_ref)


def ffn_experts(xs, tile_expert, tile_valid, w1, w3, w2, tm, tf=1792):
    N, D = xs.shape
    F = w1.shape[2]
    grid_spec = pltpu.PrefetchScalarGridSpec(
        num_scalar_prefetch=2,
        grid=(N // tm, F // tf),
        in_specs=[pl.BlockSpec((tm, D), lambda i, f, te, tv: (i, 0)),
                  pl.BlockSpec((1, D, tf), lambda i, f, te, tv: (te[i], 0, f)),
                  pl.BlockSpec((1, D, tf), lambda i, f, te, tv: (te[i], 0, f)),
                  pl.BlockSpec((1, tf, D), lambda i, f, te, tv: (te[i], f, 0))],
        out_specs=pl.BlockSpec((tm, D), lambda i, f, te, tv: (i, 0)),
        scratch_shapes=[pltpu.VMEM((tm, D), BF16), pltpu.VMEM((tm, D), F32)],
    )
    return pl.pallas_call(
        _ffn_expert_body,
        name="ffn_experts",
        grid_spec=grid_spec,
        out_shape=jax.ShapeDtypeStruct((N, D), F32),
        compiler_params=_params("arbitrary", "arbitrary"),
    )(tile_expert, tile_valid, xs, w1, w3, w2)


def _router_body(x_ref, g_ref, wr_ref, hn_ref, route_ref):
    hn = _rms(x_ref[...], g_ref[...])
    hn_ref[...] = hn
    xh, xm, xl = _split3(hn)
    wh, wm, wl = wr_ref[0], wr_ref[1], wr_ref[2]
    d = lambda a, b: jnp.dot(a, b, preferred_element_type=F32)
    logits = d(xh, wh) + (d(xh, wm) + d(xm, wh)) + (d(xm, wm) + d(xh, wl) + d(xl, wh))
    lane = lax.broadcasted_iota(jnp.int32, logits.shape, 1)
    lane_f = lane.astype(F32)
    neg = -jnp.inf
    lg = jnp.where(lane < N_EXPERTS, logits, neg)
    m1 = jnp.max(lg, axis=-1, keepdims=True)
    i1 = jnp.min(jnp.where(lg == m1, lane_f, 128.0), axis=-1, keepdims=True)
    lg2 = jnp.where(lane_f == i1, neg, lg)
    m2 = jnp.max(lg2, axis=-1, keepdims=True)
    i2 = jnp.min(jnp.where(lg2 == m2, lane_f, 128.0), axis=-1, keepdims=True)
    e = jnp.exp(m2 - m1)
    den = 1.0 + e
    g1 = 1.0 / den
    g2 = e / den
    route = jnp.where(lane == 0, g1, jnp.where(lane == 1, g2, jnp.where(lane == 2, i1, jnp.where(lane == 3, i2, 0.0))))
    route_ref[...] = route


def router(x, norm_g, w_router, tm=512):
    M, D = x.shape
    tm = min(tm, M)
    wr = jnp.pad(w_router.astype(F32), ((0, 0), (0, 128 - N_EXPERTS)))
    wh = wr.astype(BF16)
    r1 = wr - wh.astype(F32)
    wm = r1.astype(BF16)
    wl = (r1 - wm.astype(F32)).astype(BF16)
    w3 = jnp.stack([wh, wm, wl])
    return pl.pallas_call(
        _router_body,
        name="router",
        grid=(M // tm,),
        in_specs=[pl.BlockSpec((tm, D), lambda i: (i, 0)), pl.BlockSpec((1, D), lambda i: (0, 0)),
                  pl.BlockSpec((3, D, 128), lambda i: (0, 0, 0))],
        out_specs=[pl.BlockSpec((tm, D), lambda i: (i, 0)), pl.BlockSpec((tm, 128), lambda i: (i, 0))],
        out_shape=[jax.ShapeDtypeStruct((M, D), F32), jax.ShapeDtypeStruct((M, 128), F32)],
        compiler_params=_params("parallel"),
    )(x, norm_g.reshape(1, D), w3)


def _row_put(src_ref, r, dst_hbm, idx, sem):
    return pltpu.make_async_copy(src_ref.at[pl.ds(r, 1), :], dst_hbm.at[pl.ds(idx, 1), :], sem)


def _row_get(src_hbm, idx, dst_ref, r, sem):
    return pltpu.make_async_copy(src_hbm.at[pl.ds(idx, 1), :], dst_ref.at[pl.ds(r, 1), :], sem)


def _dispatch_body(i1_ref, i2_ref, hn_ref, xs_in_hbm, xs_hbm, sem):
    del xs_in_hbm
    tm = hn_ref.shape[0]

    def start(r, _):
        _row_put(hn_ref, r, xs_hbm, i1_ref[0, 0, r], sem).start()
        _row_put(hn_ref, r, xs_hbm, i2_ref[0, 0, r], sem).start()
        return 0

    lax.fori_loop(0, tm, start, 0, unroll=8)

    def wait(r, _):
        _row_put(hn_ref, r, xs_hbm, 0, sem).wait()
        _row_put(hn_ref, r, xs_hbm, 0, sem).wait()
        return 0

    lax.fori_loop(0, tm, wait, 0, unroll=8)


def dispatch_rows(hn, pos1, pos2, xs, tm=512):
    M, D = hn.shape
    tm = min(tm, M)
    idx_spec = pl.BlockSpec((1, 1, tm), lambda i: (i, 0, 0), memory_space=pltpu.SMEM)
    return pl.pallas_call(
        _dispatch_body,
        name="dispatch_rows",
        grid=(M // tm,),
        in_specs=[idx_spec, idx_spec, pl.BlockSpec((tm, D), lambda i: (i, 0)), pl.BlockSpec(memory_space=pl.ANY)],
        out_specs=pl.BlockSpec(memory_space=pl.ANY),
        out_shape=jax.ShapeDtypeStruct(xs.shape, xs.dtype),
        input_output_aliases={3: 0},
        scratch_shapes=[pltpu.SemaphoreType.DMA(())],
        compiler_params=_params("arbitrary"),
    )(pos1.reshape(M // tm, 1, tm), pos2.reshape(M // tm, 1, tm), hn, xs)


def _combine_body(i1_ref, i2_ref, x_ref, route_ref, ys_hbm, o_ref, b1, b2, sem1, sem2):
    tm = o_ref.shape[0]

    def start(r, _):
        _row_get(ys_hbm, i1_ref[0, 0, r], b1, r, sem1).start()
        _row_get(ys_hbm, i2_ref[0, 0, r], b2, r, sem2).start()
        return 0

    lax.fori_loop(0, tm, start, 0, unroll=8)

    def wait(r, _):
        _row_get(ys_hbm, 0, b1, r, sem1).wait()
        _row_get(ys_hbm, 0, b2, r, sem2).wait()
        return 0

    lax.fori_loop(0, tm, wait, 0, unroll=8)
    o_ref[...] = x_ref[...] + route_ref[:, 0:1] * b1[...] + route_ref[:, 1:2] * b2[...]


def combine_rows(x, route, ys, pos1, pos2, tm=256):
    M, D = x.shape
    tm = min(tm, M)
    idx_spec = pl.BlockSpec((1, 1, tm), lambda i: (i, 0, 0), memory_space=pltpu.SMEM)
    return pl.pallas_call(
        _combine_body,
        name="combine_rows",
        grid=(M // tm,),
        in_specs=[idx_spec, idx_spec, pl.BlockSpec((tm, D), lambda i: (i, 0)),
                  pl.BlockSpec((tm, 128), lambda i: (i, 0)), pl.BlockSpec(memory_space=pl.ANY)],
        out_specs=pl.BlockSpec((tm, D), lambda i: (i, 0)),
        out_shape=jax.ShapeDtypeStruct((M, D), F32),
        scratch_shapes=[pltpu.VMEM((tm, D), F32), pltpu.VMEM((tm, D), F32),
                        pltpu.SemaphoreType.DMA(()), pltpu.SemaphoreType.DMA(())],
        compiler_params=_params("arbitrary"),
    )(pos1.reshape(M // tm, 1, tm), pos2.reshape(M // tm, 1, tm), x, route, ys)


def moe_ffn(xs_list, norm_g, w_router, w1, w3, w2, tm_e):
    D = xs_list[0].shape[1]
    routed = [router(x, norm_g, w_router) for x in xs_list]
    top_i = jnp.concatenate([route[:, 2:4] for _, route in routed], axis=0).astype(jnp.int32)
    T = top_i.shape[0]
    e_flat = top_i.reshape(-1)
    onehot = (e_flat[:, None] == jnp.arange(N_EXPERTS, dtype=jnp.int32)[None, :]).astype(jnp.int32)
    csum = jnp.cumsum(onehot, axis=0)
    counts = csum[-1]
    rank = jnp.take_along_axis(csum, e_flat[:, None], axis=1)[:, 0] - 1
    ptiles = (counts + tm_e - 1) // tm_e
    tile_end = jnp.cumsum(ptiles)
    pos = (tile_end - ptiles)[e_flat] * tm_e + rank
    pos1, pos2 = pos[0::2], pos[1::2]
    n_tiles = (2 * T) // tm_e + N_EXPERTS
    tile_ids = jnp.arange(n_tiles, dtype=jnp.int32)
    tile_valid = (tile_ids < tile_end[-1]).astype(jnp.int32)
    last_e = jnp.max(jnp.where(counts > 0, jnp.arange(N_EXPERTS, dtype=jnp.int32), 0))
    tile_expert = jnp.sum((tile_end[None, :] <= tile_ids[:, None]).astype(jnp.int32), axis=1)
    tile_expert = jnp.minimum(tile_expert, last_e)
    xs = jnp.zeros((n_tiles * tm_e, D), F32)
    t0 = 0
    for hn, _ in routed:
        t1 = t0 + hn.shape[0]
        xs = dispatch_rows(hn, pos1[t0:t1], pos2[t0:t1], xs)
        t0 = t1
    ys = ffn_experts(xs, tile_expert, tile_valid, w1, w3, w2, tm_e)
    outs = []
    t0 = 0
    for x, (_, route) in zip(xs_list, routed):
        t1 = t0 + x.shape[0]
        outs.append(combine_rows(x, route, ys, pos1[t0:t1], pos2[t0:t1]))
        t0 = t1
    return outs


def _rope_lanes(v, cos, sin, lane):
    partner = jnp.where(lane < 80, pltpu.roll(v, 112, 1), pltpu.roll(v, 16, 1))
    return v * cos + partner * sin


def _mla_rows_body(x_ref, g_ref, wd_ref, qln_ref, kvln_ref, krn_ref, wuq_ref, qn_ref, cos_ref, sin_ref,
                   q_ref, ckv_ref, kr_ref):
    hn = _rms(x_ref[...], g_ref[...]).astype(BF16)
    d = jnp.dot(hn, wd_ref[...], preferred_element_type=F32)
    cq = _rms(d[:, 0:MLA_Q_LORA], qln_ref[...]).astype(BF16)
    ckv_ref[...] = _rms(d[:, MLA_Q_LORA:MLA_Q_LORA + MLA_KV_LORA], kvln_ref[...])
    cos = cos_ref[...]
    sin = sin_ref[...]
    lane = lax.broadcasted_iota(jnp.int32, cos.shape, 1)
    kr = d[:, MLA_Q_LORA + MLA_KV_LORA:]
    kr = kr * lax.rsqrt(jnp.sum(kr * kr, axis=-1, keepdims=True) * (1.0 / MLA_ROPE) + EPS) * krn_ref[...]
    kr_ref[...] = _rope_lanes(kr, cos, sin, lane)
    q = jnp.dot(cq, wuq_ref[...], preferred_element_type=F32)
    nope = lane < MLA_NOPE
    for h in range(MLA_HEADS):
        qh = q[:, h * 128:(h + 1) * 128]
        sq = qh * qh
        ss_n = jnp.sum(jnp.where(nope, sq, 0.0), axis=-1, keepdims=True) * (1.0 / MLA_NOPE)
        ss_r = jnp.sum(jnp.where(nope, 0.0, sq), axis=-1, keepdims=True) * (1.0 / MLA_ROPE)
        r = jnp.where(nope, lax.rsqrt(ss_n + EPS), lax.rsqrt(ss_r + EPS))
        qh = _rope_lanes(qh * r * qn_ref[...], cos, sin, lane)
        q_ref[:, h * 128:(h + 1) * 128] = (qh * (MLA_SCALE * math.log2(math.e))).astype(BF16)


def mla_rows(x, norm_g, wd_pad, q_lora_norm, kv_lora_norm, krn_pad, wuq_pad, qn_pad, cos_t, sin_t, tm=512):
    M, D = x.shape
    tm = min(tm, M)
    const = lambda shape: pl.BlockSpec(shape, lambda i: (0,) * len(shape))
    rows = lambda n: pl.BlockSpec((tm, n), lambda i: (i, 0))
    table_tiles = cos_t.shape[0] // tm
    table = pl.BlockSpec((tm, 128), lambda i: (i % table_tiles, 0))
    return pl.pallas_call(
        _mla_rows_body,
        name="mla_rows",
        grid=(M // tm,),
        in_specs=[rows(D), const((1, D)), const(wd_pad.shape), const((1, MLA_Q_LORA)), const((1, MLA_KV_LORA)),
                  const((1, 128)), const(wuq_pad.shape), const((1, 128)), table, table],
        out_specs=[rows(MLA_HEADS * 128), rows(MLA_KV_LORA), rows(128)],
        out_shape=[jax.ShapeDtypeStruct((M, MLA_HEADS * 128), BF16), jax.ShapeDtypeStruct((M, MLA_KV_LORA), F32),
                   jax.ShapeDtypeStruct((M, 128), F32)],
        compiler_params=_params("parallel"),
    )(x, norm_g.reshape(1, D), wd_pad, q_lora_norm.reshape(1, -1), kv_lora_norm.reshape(1, -1), krn_pad,
      wuq_pad, qn_pad, cos_t, sin_t)


def _mla_kv_body(ckv_ref, kr_ref, wuk_ref, wuv_ref, kn_ref, k_ref, v_ref):
    c = ckv_ref[...].astype(BF16)
    kn = jnp.dot(c, wuk_ref[...], preferred_element_type=F32)
    lane = lax.broadcasted_iota(jnp.int32, (1, MLA_HEADS * 128), 1)
    ones_col = jnp.where(jnp.bitwise_and(lane, 127) == MLA_V, 1.0, 0.0)
    v_ref[...] = (jnp.dot(c, wuv_ref[...], preferred_element_type=F32) + ones_col).astype(BF16)
    kr = kr_ref[...]
    for h in range(MLA_HEADS):
        kh = kn[:, h * 128:(h + 1) * 128]
        r = lax.rsqrt(jnp.sum(kh * kh, axis=-1, keepdims=True) * (1.0 / MLA_NOPE) + EPS)
        k_ref[:, h * 128:(h + 1) * 128] = (kh * r * kn_ref[...] + kr).astype(BF16)


def mla_kv(ckv, kr_pad, wuk_pad, wuv, kn_pad, tm=512):
    M = ckv.shape[0]
    tm = min(tm, M)
    const = lambda shape: pl.BlockSpec(shape, lambda i: (0,) * len(shape))
    rows = lambda n: pl.BlockSpec((tm, n), lambda i: (i, 0))
    return pl.pallas_call(
        _mla_kv_body,
        name="mla_kv",
        grid=(M // tm,),
        in_specs=[rows(MLA_KV_LORA), rows(128), const(wuk_pad.shape), const(wuv.shape), const((1, 128))],
        out_specs=[rows(MLA_HEADS * 128), rows(MLA_HEADS * 128)],
        out_shape=[jax.ShapeDtypeStruct((M, MLA_HEADS * 128), BF16),
                   jax.ShapeDtypeStruct((M, MLA_HEADS * 128), BF16)],
        compiler_params=_params("parallel"),
    )(ckv, kr_pad, wuk_pad, wuv, kn_pad)


def _flash_body(qi_ref, ki_ref, q_ref, k_ref, v_ref, *rest, causal):
    if causal:
        bias_ref, o_ref, m_ref, acc_ref = rest
    else:
        o_ref, m_ref, acc_ref = rest
    t = pl.program_id(2)
    qi = qi_ref[t]
    ki = ki_ref[t]
    tq = q_ref.shape[1]

    @pl.when(ki == 0)
    def _():
        m_ref[...] = jnp.full_like(m_ref, -jnp.inf)
        acc_ref[...] = jnp.zeros_like(acc_ref)

    def step(masked):
        tk = k_ref.shape[1]
        sub = min(FLASH_KV_SUB, tk) if masked else tk
        for h in range(2):
            hl = slice(h * 128, (h + 1) * 128)
            m, acc = m_ref[h], acc_ref[h]
            for j in range(tk // sub):
                r0 = j * sub if masked else 0
                cols = slice(j * sub, (j + 1) * sub)
                s = lax.dot_general(q_ref[0, r0:, hl], k_ref[0, cols, hl], NT_DIMS, preferred_element_type=F32)
                if masked:
                    s = s + bias_ref[r0:, cols]
                m_new = jnp.maximum(m[r0:], jnp.broadcast_to(jnp.max(s, axis=-1, keepdims=True), (tq - r0, 128)))
                alpha = jnp.exp2(m[r0:] - m_new)
                m_wide = jnp.concatenate([m_new] * (sub // 128), axis=1) if sub % 128 == 0 else m_new[:, 0:1]
                p = jnp.exp2(s - m_wide)
                acc_new = alpha * acc[r0:] + jnp.dot(p.astype(BF16), v_ref[0, cols, hl], preferred_element_type=F32)
                if r0:
                    m_new = jnp.concatenate([m[:r0], m_new], axis=0)
                    acc_new = jnp.concatenate([acc[:r0], acc_new], axis=0)
                m, acc = m_new, acc_new
            m_ref[h], acc_ref[h] = m, acc

    def finish():
        lane = lax.broadcasted_iota(jnp.int32, (tq, 128), 1)
        o0 = acc_ref[0] / acc_ref[0, :, MLA_V:MLA_V + 1]
        o1 = acc_ref[1] / acc_ref[1, :, MLA_V:MLA_V + 1]
        o_ref[0] = jnp.where(lane < MLA_V, o0, pltpu.roll(o1, MLA_V, 1)).astype(o_ref.dtype)

    if causal:
        @pl.when(ki < qi)
        def _():
            step(False)

        @pl.when(ki == qi)
        def _():
            step(True)
            finish()
    else:
        step(False)

        @pl.when(t == pl.num_programs(2) - 1)
        def _():
            finish()


def mla_attention(q, k, v, causal, tq, tk):
    B, Lq, _ = q.shape
    Lk = k.shape[1]
    nq, nk = Lq // tq, Lk // tk
    pairs = [(i, j) for i in range(nq) for j in range(nk) if (j <= i or not causal)]
    qi_tab = jnp.asarray([i for i, _ in pairs], jnp.int32)
    ki_tab = jnp.asarray([j for _, j in pairs], jnp.int32)
    in_specs = [pl.BlockSpec((1, tq, 256), lambda b, p, t, qt, kt: (b, qt[t], p)),
                pl.BlockSpec((1, tk, 256), lambda b, p, t, qt, kt: (b, kt[t], p)),
                pl.BlockSpec((1, tk, 256), lambda b, p, t, qt, kt: (b, kt[t], p))]
    args = [q, k, v]
    if causal:
        chunk = jnp.arange(tq) // CHUNK
        bias = jnp.where(chunk[None, :] <= chunk[:, None], 0.0, -jnp.inf).astype(F32)
        in_specs.append(pl.BlockSpec((tq, tk), lambda b, p, t, qt, kt: (0, 0)))
        args.append(bias)
    grid_spec = pltpu.PrefetchScalarGridSpec(
        num_scalar_prefetch=2,
        grid=(B, MLA_HEADS // 2, len(pairs)),
        in_specs=in_specs,
        out_specs=pl.BlockSpec((1, tq, 128), lambda b, p, t, qt, kt: (b, qt[t], p)),
        scratch_shapes=[pltpu.VMEM((2, tq, 128), F32), pltpu.VMEM((2, tq, 128), F32)],
    )
    return pl.pallas_call(
        functools.partial(_flash_body, causal=causal),
        name="mla_flash",
        grid_spec=grid_spec,
        out_shape=jax.ShapeDtypeStruct((B, Lq, MLA_HEADS * MLA_V), BF16),
        compiler_params=_params("parallel", "parallel", "arbitrary"),
    )(qi_tab, ki_tab, *args)


def _rope_tables(pos):
    half = MLA_ROPE // 2
    inv = ROPE_THETA ** (-jnp.arange(half, dtype=F32) / half)
    ang = pos.astype(F32)[:, None] * inv[None, :]
    cos, sin = jnp.cos(ang), jnp.sin(ang)
    n = pos.shape[0]
    ones = jnp.ones((n, MLA_NOPE), F32)
    zeros = jnp.zeros((n, 128 - MLA_NOPE - MLA_ROPE), F32)
    cos_t = jnp.concatenate([ones, cos, cos, zeros], axis=1)
    sin_t = jnp.concatenate([jnp.zeros((n, MLA_NOPE), F32), -sin, sin, zeros], axis=1)
    return cos_t, sin_t


def _pad_heads(w, d_in, d_head, d_pad):
    w = w.reshape(d_in, MLA_HEADS, d_head)
    return jnp.pad(w, ((0, 0), (0, 0), (0, d_pad - d_head))).reshape(d_in, MLA_HEADS * d_pad)


def mla_weights(norm_g, w_down, q_lora_norm, kv_lora_norm, w_uq, w_ukv, q_nope_norm, q_rope_norm, k_nope_norm,
                k_rope_norm, w_o):
    D = w_down.shape[0]
    split = MLA_Q_LORA + MLA_KV_LORA
    tail = 128 - MLA_NOPE - MLA_ROPE
    wd_pad = jnp.concatenate([w_down[:, :split], jnp.zeros((D, MLA_NOPE), F32), w_down[:, split:],
                              jnp.zeros((D, tail), F32)], axis=1).astype(BF16)
    lane_pad = lambda a, b: jnp.concatenate([a, b, jnp.zeros((128 - a.shape[0] - b.shape[0],), F32)]).reshape(1, 128)
    krn_pad = lane_pad(jnp.zeros((MLA_NOPE,), F32), k_rope_norm)
    qn_pad = lane_pad(q_nope_norm, q_rope_norm)
    kn_pad = lane_pad(k_nope_norm, jnp.zeros((0,), F32))
    wuq_pad = _pad_heads(w_uq, MLA_Q_LORA, MLA_NOPE + MLA_ROPE, 128).astype(BF16)
    wukv = w_ukv.reshape(MLA_KV_LORA, MLA_HEADS, MLA_NOPE + MLA_V)
    wuk_pad = _pad_heads(wukv[:, :, :MLA_NOPE].reshape(MLA_KV_LORA, -1), MLA_KV_LORA, MLA_NOPE, 128).astype(BF16)
    wuv = _pad_heads(wukv[:, :, MLA_NOPE:].reshape(MLA_KV_LORA, -1), MLA_KV_LORA, MLA_V, 128).astype(BF16)
    rows_w = (norm_g, wd_pad, q_lora_norm, kv_lora_norm, krn_pad, wuq_pad, qn_pad)
    return rows_w, (wuk_pad, wuv, kn_pad), w_o.astype(BF16)


def mla_block(x, b, l, mw, ckv_past, kr_past, tile):
    rows_w, kv_w, w_o = mw
    n_past = 0 if ckv_past is None else ckv_past.shape[1]
    cos_t, sin_t = _rope_tables(n_past + jnp.arange(l))
    reps = max(1, min(b * l, 512) // l)
    q, ckv, kr = mla_rows(x, *rows_w, jnp.tile(cos_t, (reps, 1)), jnp.tile(sin_t, (reps, 1)))
    kr_out = kr[:, MLA_NOPE:MLA_NOPE + MLA_ROPE]
    if ckv_past is None:
        k_all, v_all = mla_kv(ckv, kr, *kv_w)
        o = mla_attention(q.reshape(b, l, -1), k_all.reshape(b, l, -1), v_all.reshape(b, l, -1), True, tile, tile)
    else:
        lk = n_past + l
        ckv_all = jnp.concatenate([ckv_past, ckv.reshape(b, l, -1)], axis=1).reshape(b * lk, -1)
        kr_pad = jnp.pad(kr_past, ((0, 0), (0, 0), (MLA_NOPE, 128 - MLA_NOPE - MLA_ROPE)))
        kr_all = jnp.concatenate([kr_pad, kr.reshape(b, l, 128)], axis=1).reshape(b * lk, 128)
        k_all, v_all = mla_kv(ckv_all, kr_all, *kv_w, tm=lk // 5)
        o = mla_attention(q.reshape(b, l, -1), k_all.reshape(b, lk, -1), v_all.reshape(b, lk, -1), False, l, lk)
    x = rms_matmul(o.reshape(b * l, -1), w_o, res=x, tm=1024, name="mla_out_proj")
    return x, ckv, kr_out


def kernel(x_prompt, x_sample, mem_prompt, state_conv_a, state_conv_ssm, state_ssm, cache_mla_ckv, cache_mla_krope, cache_mem_k, cache_mem_v, norm_mix, norm_mem, norm_ffn, w_in_e, conv_a_w, conv_s_w, conv_s_b, dt_bias, a_log, d_skip, ssm_norm, w_out_e, w_down_o, q_lora_norm, kv_lora_norm, w_uq, w_ukv, q_nope_norm, q_rope_norm, k_nope_norm, k_rope_norm, w_o_mla, mem_src_norm, w_mem_q, w_mem_k, w_mem_v, mem_q_norm, mem_k_norm, w_mem_o, w_ffn1, w_ffn3, w_ffn2, w_router, w_exp1, w_exp3, w_exp2):
    bp, lp, D = x_prompt.shape
    bs, ls, _ = x_sample.shape
    past = cache_mla_ckv.shape[2]
    bf = lambda w: w.astype(BF16)
    xp = x_prompt.reshape(bp * lp, D)
    xs = x_sample.reshape(bs * ls, D)

    ew = even_weights(w_in_e[0], w_out_e[0], norm_mix[0], conv_a_w[0], conv_s_w[0], conv_s_b[0], dt_bias[0],
                      a_log[0], d_skip[0], ssm_norm[0])
    xp, p_ca, p_cs, p_st = even_mixer_block(
        xp, bp, lp, ew, jnp.zeros((bp, 2, A_WIDTH), F32), jnp.zeros((bp, 3, SSM_XBC), F32),
        jnp.zeros((bp, SSM_HEADS, SSM_HEAD_DIM, SSM_STATE), F32), 1024)
    xs, s_ca, s_cs, s_st = even_mixer_block(xs, bs, ls, ew, state_conv_a[0], state_conv_ssm[0], state_ssm[0], 512)

    p_mk, p_mv = [], []

    def memory_block(i, xp, xs):
        mk, mv, mk_b, mv_b = mem_kv(mem_prompt, mem_src_norm[i], bf(w_mem_k[i]), bf(w_mem_v[i]), mem_k_norm[i])
        p_mk.append(mk.reshape(bp, MEM_TOKENS, MEM_HEADS, MEM_HEAD_DIM))
        p_mv.append(mv.reshape(bp, MEM_TOKENS, MEM_HEADS, MEM_HEAD_DIM))
        wq, wo = bf(w_mem_q[i]), bf(w_mem_o[i])
        xp = mem_attend(xp, norm_mem[i], mk_b, mv_b, wq, mem_q_norm[i], wo, lp)
        ck = bf(cache_mem_k[i]).reshape(bs, MEM_TOKENS, D)
        cv = bf(cache_mem_v[i]).reshape(bs, MEM_TOKENS, D)
        xs = mem_attend(xs, norm_mem[i], ck, cv, wq, mem_q_norm[i], wo, ls)
        return xp, xs

    xp, xs = memory_block(0, xp, xs)
    w1, w3, w2 = bf(w_ffn1[0]), bf(w_ffn3[0]), bf(w_ffn2[0])
    xp = ffn_dense(xp, norm_ffn[0], w1, w3, w2)
    xs = ffn_dense(xs, norm_ffn[0], w1, w3, w2)

    mw = mla_weights(norm_mix[1], w_down_o[0], q_lora_norm[0], kv_lora_norm[0], w_uq[0], w_ukv[0], q_nope_norm[0],
                     q_rope_norm[0], k_nope_norm[0], k_rope_norm[0], w_o_mla[0])
    xp, p_ckv, p_kr = mla_block(xp, bp, lp, mw, None, None, min(lp, 1024))
    xs, s_ckv, s_kr = mla_block(xs, bs, ls, mw, cache_mla_ckv[0], cache_mla_krope[0], None)

    xp, xs = memory_block(1, xp, xs)
    we1, we3, we2 = bf(w_exp1[0]), bf(w_exp3[0]), bf(w_exp2[0])
    xp, xs = moe_ffn([xp, xs], norm_ffn[1], w_router[0], we1, we3, we2, 512)

    return (xp.reshape(bp, lp, D), xs.reshape(bs, ls, D),
            p_ca[None], p_cs[None], p_st[None],
            p_ckv.reshape(1, bp, lp, -1), p_kr.reshape(1, bp, lp, -1), jnp.stack(p_mk), jnp.stack(p_mv),
            s_ca[None], s_cs[None], s_st[None],
            s_ckv.reshape(1, bs, ls, -1), s_kr.reshape(1, bs, ls, -1))
```

```python
import functools
import math

import jax
import jax.numpy as jnp
from jax import lax
from jax.experimental import pallas as pl
from jax.experimental.pallas import tpu as pltpu

F32 = jnp.float32
BF16 = jnp.bfloat16
EPS = 1e-6

D_MODEL = 1024
CHUNK = 64
A_WIDTH = 1024
SSM_HEADS = 32
SSM_HEAD_DIM = 64
SSM_INNER = 2048
SSM_STATE = 128
SSM_GROUPS = 4
SSM_XBC = 3072
HEAD_LANES = 128
PROJ_W = 3 * A_WIDTH + SSM_INNER + SSM_XBC + HEAD_LANES
COL_Z = 3 * A_WIDTH
COL_XBC = COL_Z + SSM_INNER
COL_DT = COL_XBC + SSM_XBC
MLA_HEADS = 16
MLA_Q_LORA = 512
MLA_KV_LORA = 256
MLA_NOPE = 64
MLA_ROPE = 32
MLA_V = 64
MLA_SCALE = (MLA_NOPE + MLA_ROPE) ** -0.5
ROPE_THETA = 10000.0
MEM_TOKENS = 256
MEM_HEADS = 4
MEM_HEAD_DIM = 256
N_EXPERTS = 8
VMEM_LIMIT = 56 * 1024 * 1024
MIXER_CHUNKS_PER_STEP = 4
FLASH_KV_SUB = 256
NT_DIMS = (((1,), (1,)), ((), ()))


def _params(*sem):
    return pltpu.CompilerParams(dimension_semantics=sem, vmem_limit_bytes=VMEM_LIMIT)


def _rms(x, g):
    return x * lax.rsqrt(jnp.mean(x * x, axis=-1, keepdims=True) + EPS) * g


def _split3(x):
    hi = x.astype(BF16)
    r1 = x - hi.astype(F32)
    mid = r1.astype(BF16)
    lo = (r1 - mid.astype(F32)).astype(BF16)
    return hi, mid, lo


def _dot_sel(x, sel_bf16):
    hi, mid, lo = _split3(x)
    d = lambda a: jnp.dot(a, sel_bf16, preferred_element_type=F32)
    return d(hi) + d(mid) + d(lo)


def _dot_sel2(x, sel_bf16):
    hi = x.astype(BF16)
    mid = (x - hi.astype(F32)).astype(BF16)
    d = lambda a: jnp.dot(a, sel_bf16, preferred_element_type=F32)
    return d(hi) + d(mid)


def _sel_dot_nt(sel_bf16, x):
    hi, mid, lo = _split3(x)
    d = lambda a: lax.dot_general(sel_bf16, a, NT_DIMS, preferred_element_type=F32)
    return d(hi) + d(mid) + d(lo)


def _sel_dot(sel_bf16, x):
    hi, mid, lo = _split3(x)
    d = lambda a: jnp.dot(sel_bf16, a, preferred_element_type=F32)
    return d(hi) + d(mid) + d(lo)


def _rms_matmul_body(*refs, use_norm, use_res):
    it = iter(refs)
    x_ref = next(it)
    g_ref = next(it) if use_norm else None
    w_ref = next(it)
    r_ref = next(it) if use_res else None
    o_ref = next(it)
    xn_ref = next(it) if use_norm else None
    if use_norm:
        @pl.when(pl.program_id(1) == 0)
        def _():
            xn_ref[...] = _rms(x_ref[...], g_ref[...]).astype(BF16)
        xb = xn_ref[...]
    else:
        xb = x_ref[...]
    acc = jnp.dot(xb, w_ref[...], preferred_element_type=F32)
    if use_res:
        acc = acc + r_ref[...]
    o_ref[...] = acc.astype(o_ref.dtype)


def rms_matmul(x, w, g=None, res=None, out_dtype=F32, tm=512, tn=None, name="rms_matmul"):
    M, K = x.shape
    N = w.shape[1]
    tn = tn or N
    tm = min(tm, M)
    in_specs = [pl.BlockSpec((tm, K), lambda i, j: (i, 0))]
    args = [x]
    if g is not None:
        in_specs.append(pl.BlockSpec((1, K), lambda i, j: (0, 0)))
        args.append(g.reshape(1, K))
    in_specs.append(pl.BlockSpec((K, tn), lambda i, j: (0, j)))
    args.append(w)
    if res is not None:
        in_specs.append(pl.BlockSpec((tm, tn), lambda i, j: (i, j)))
        args.append(res)
    scratch = [pltpu.VMEM((tm, K), BF16)] if g is not None else []
    return pl.pallas_call(
        functools.partial(_rms_matmul_body, use_norm=g is not None, use_res=res is not None),
        name=name,
        grid=(M // tm, N // tn),
        in_specs=in_specs,
        out_specs=pl.BlockSpec((tm, tn), lambda i, j: (i, j)),
        out_shape=jax.ShapeDtypeStruct((M, N), out_dtype),
        scratch_shapes=scratch,
        compiler_params=_params("parallel", "arbitrary"),
    )(*args)


def _softplus(v):
    return jnp.maximum(v, 0.0) + jnp.log1p(jnp.exp(-jnp.abs(v)))


def _mixer_body(proj_ref, wa_ref, ws_ref, bs_ref, dtb_ref, alog_ref, dskip_ref, gn_ref,
                tri_ref, selrow_ref, expand_ref, eye_ref,
                ca_prev_ref, cs_prev_ref, st_prev_ref,
                mix_ref, ca_out_ref, cs_out_ref, st_ref,
                abuf, sbuf):
    c = pl.program_id(1)
    Q = CHUNK
    R = proj_ref.shape[1]

    @pl.when(c == 0)
    def _():
        abuf[0:8, :] = ca_prev_ref[0]
        sbuf[0:8, :] = cs_prev_ref[0]
        st_ref[0] = st_prev_ref[0]

    g_b = proj_ref[0, :, 0:A_WIDTH]
    abuf[8:8 + R, :] = proj_ref[0, :, A_WIDTH:2 * A_WIDTH] * proj_ref[0, :, 2 * A_WIDTH:3 * A_WIDTH]
    conv_a = wa_ref[0:1, :] * abuf[6:6 + R, :]
    conv_a = conv_a + wa_ref[1:2, :] * abuf[7:7 + R, :]
    conv_a = conv_a + wa_ref[2:3, :] * abuf[8:8 + R, :]
    mix_ref[0, :, 0:A_WIDTH] = (g_b * conv_a).astype(BF16)
    tail_a = abuf[R:R + 8, :]
    ca_out_ref[0] = tail_a
    abuf[0:8, :] = tail_a

    sbuf[8:8 + R, :] = proj_ref[0, :, COL_XBC:COL_DT]
    xc_all = ws_ref[0:1, :] * sbuf[5:5 + R, :]
    xc_all = xc_all + ws_ref[1:2, :] * sbuf[6:6 + R, :]
    xc_all = xc_all + ws_ref[2:3, :] * sbuf[7:7 + R, :]
    xc_all = xc_all + ws_ref[3:4, :] * sbuf[8:8 + R, :]
    xc_all = xc_all + bs_ref[...]
    xc_all = xc_all * jax.nn.sigmoid(xc_all)
    tail_s = sbuf[R:R + 8, :]
    cs_out_ref[0] = tail_s
    sbuf[0:8, :] = tail_s

    dt_all = _softplus(proj_ref[0, :, COL_DT:COL_DT + HEAD_LANES] + dtb_ref[...])
    la_all = dt_all * (-jnp.exp(alog_ref[...]))

    lane = lax.broadcasted_iota(jnp.int32, (Q, 128), 1)
    row = lax.broadcasted_iota(jnp.int32, (Q, 128), 0)
    tril2 = jnp.where(lane >= Q, lane - Q, lane) <= row
    left = lane < Q
    eye = eye_ref[...]

    for ci in range(R // Q):
        _ssd_chunk(slice(ci * Q, (ci + 1) * Q), xc_all, dt_all, la_all, tril2, left, eye,
                   proj_ref, dskip_ref, gn_ref, tri_ref, selrow_ref, expand_ref, mix_ref, st_ref)


def _ssd_chunk(rows, xc_all, dt_all, la_all, tril2, left, eye,
               proj_ref, dskip_ref, gn_ref, tri_ref, selrow_ref, expand_ref, mix_ref, st_ref):
    Q = CHUNK
    xc = xc_all[rows]
    dt = dt_all[rows]
    cs = _sel_dot(tri_ref[...], la_all[rows])
    cs_last = cs[Q - 1:Q, :]
    w_st = dt * jnp.exp(cs_last - cs)
    cs_x = _dot_sel(cs, expand_ref[...])
    w_st_x = _dot_sel2(w_st, expand_ref[...])
    e_in_x = jnp.exp(cs_x)
    dec_x = e_in_x[Q - 1:Q, :]
    cs_r = _sel_dot_nt(selrow_ref[...], cs)
    dt_r = _sel_dot_nt(selrow_ref[...], dt)
    cs_rp = jnp.concatenate([cs_r[0:16], cs_r[16:32]], axis=1)
    dt_rp = jnp.concatenate([dt_r[0:16], dt_r[16:32]], axis=1)

    for g in range(SSM_GROUPS):
        gl = slice(SSM_INNER + g * SSM_STATE, SSM_INNER + (g + 1) * SSM_STATE)
        Bg = xc[:, gl].astype(BF16)
        Cg = xc[:, gl.start + SSM_GROUPS * SSM_STATE: gl.stop + SSM_GROUPS * SSM_STATE].astype(BF16)
        sc = lax.dot_general(Cg, Bg, NT_DIMS, preferred_element_type=F32)
        sc2 = jnp.concatenate([sc, sc], axis=1)
        BgT = lax.dot_general(eye, Bg, NT_DIMS, preferred_element_type=F32).astype(BF16)
        hs = slice(g * 512, (g + 1) * 512)
        st_g = st_ref[0, :, hs]
        y_off = jnp.dot(Cg, st_g.astype(BF16), preferred_element_type=F32)
        xw = (xc[:, hs] * w_st_x[:, hs]).astype(BF16)
        st_ref[0, :, hs] = dec_x[:, hs] * st_g + jnp.dot(BgT, xw, preferred_element_type=F32)
        ys = []
        for kk in range(4):
            k = 4 * g + kk
            pl_ = slice(k * 128, (k + 1) * 128)
            diff = cs_x[:, pl_] - cs_rp[k:k + 1, :]
            decay = jnp.exp(jnp.where(tril2, diff, -jnp.inf))
            m_pair = (sc2 * decay * dt_rp[k:k + 1, :]).astype(BF16)
            xp = xc[:, pl_]
            rhs = jnp.concatenate([jnp.where(left, xp, 0.0), jnp.where(left, 0.0, xp)], axis=0).astype(BF16)
            y_diag = jnp.dot(m_pair, rhs, preferred_element_type=F32)
            y = y_diag + y_off[:, kk * 128:(kk + 1) * 128] * e_in_x[:, pl_]
            y = y + dskip_ref[:, pl_] * xp
            z = proj_ref[0, rows, COL_Z + k * 128: COL_Z + (k + 1) * 128]
            ys.append(y * (z * jax.nn.sigmoid(z)))
        yg = jnp.concatenate(ys, axis=1)
        yn = yg * lax.rsqrt(jnp.mean(yg * yg, axis=-1, keepdims=True) + EPS) * gn_ref[:, hs]
        mix_ref[0, rows, A_WIDTH + g * 512: A_WIDTH + (g + 1) * 512] = yn.astype(BF16)


def mixer_even(proj, conv_a_w, conv_s_w, conv_s_b, dt_bias, a_log, d_skip, ssm_norm,
               conv_a_prev8, conv_s_prev8, ssm_prev_t):
    B, L, _ = proj.shape
    rows = CHUNK * MIXER_CHUNKS_PER_STEP if L % (CHUNK * MIXER_CHUNKS_PER_STEP) == 0 else CHUNK
    pad_h =lambda v: jnp.pad(v.reshape(1, SSM_HEADS), ((0, 0), (0, HEAD_LANES - SSM_HEADS)))
    tri = (jnp.arange(CHUNK)[:, None] >= jnp.arange(CHUNK)[None, :]).astype(BF16)
    heads = jnp.arange(HEAD_LANES)
    order = jnp.concatenate([jnp.arange(0, SSM_HEADS, 2), jnp.arange(1, SSM_HEADS, 2)])
    selrow = (order[:, None] == heads[None, :]).astype(BF16)
    expand = (heads[:, None] == (jnp.arange(SSM_INNER) // SSM_HEAD_DIM)[None, :]).astype(BF16)
    eye = jnp.eye(SSM_STATE, dtype=BF16)
    dskip_x = jnp.repeat(d_skip.astype(F32), SSM_HEAD_DIM).reshape(1, SSM_INNER)
    const = lambda shape: pl.BlockSpec(shape, lambda b, c: (0,) * len(shape))
    per_b = lambda shape: pl.BlockSpec((1,) + shape, lambda b, c: (b,) + (0,) * len(shape))
    return pl.pallas_call(
        _mixer_body,
        name="ssd_mixer",
        grid=(B, L // rows),
        in_specs=[
            pl.BlockSpec((1, rows, PROJ_W), lambda b, c: (b, c, 0)),
            const((3, A_WIDTH)), const((4, SSM_XBC)), const((1, SSM_XBC)),
            const((1, HEAD_LANES)), const((1, HEAD_LANES)), const((1, SSM_INNER)), const((1, SSM_INNER)),
            const((CHUNK, CHUNK)), const((SSM_HEADS, HEAD_LANES)), const((HEAD_LANES, SSM_INNER)),
            const((SSM_STATE, SSM_STATE)),
            per_b((8, A_WIDTH)), per_b((8, SSM_XBC)), per_b((SSM_STATE, SSM_INNER)),
        ],
        out_specs=[
            pl.BlockSpec((1, rows, A_WIDTH + SSM_INNER), lambda b, c: (b, c, 0)),
            per_b((8, A_WIDTH)), per_b((8, SSM_XBC)), per_b((SSM_STATE, SSM_INNER)),
        ],
        out_shape=[
            jax.ShapeDtypeStruct((B, L, A_WIDTH + SSM_INNER), BF16),
            jax.ShapeDtypeStruct((B, 8, A_WIDTH), F32),
            jax.ShapeDtypeStruct((B, 8, SSM_XBC), F32),
            jax.ShapeDtypeStruct((B, SSM_STATE, SSM_INNER), F32),
        ],
        scratch_shapes=[
            pltpu.VMEM((rows + 8, A_WIDTH), F32),
            pltpu.VMEM((rows + 8, SSM_XBC), F32),
        ],
        compiler_params=_params("parallel", "arbitrary"),
    )(proj, conv_a_w, conv_s_w, conv_s_b.reshape(1, SSM_XBC), pad_h(dt_bias), pad_h(a_log), dskip_x,
      ssm_norm.reshape(1, SSM_INNER), tri, selrow, expand, eye, conv_a_prev8, conv_s_prev8, ssm_prev_t)


def even_weights(w_in, w_out, norm_g, conv_a_w, conv_s_w, conv_s_b, dt_bias, a_log, d_skip, ssm_norm):
    w_in_pad = jnp.pad(w_in, ((0, 0), (0, PROJ_W - w_in.shape[1]))).astype(BF16)
    return (norm_g, w_in_pad, w_out.astype(BF16), (conv_a_w, conv_s_w, conv_s_b, dt_bias, a_log, d_skip, ssm_norm))


def even_mixer_block(x, b, l, ew, ca_prev, cs_prev, st_prev, tm):
    norm_g, w_in_pad, w_out, mixer_w = ew
    proj = rms_matmul(x, w_in_pad, g=norm_g, tm=tm, tn=PROJ_W // 5, name="mixer_in_proj")
    ca8 = jnp.pad(ca_prev, ((0, 0), (6, 0), (0, 0)))
    cs8 = jnp.pad(cs_prev, ((0, 0), (5, 0), (0, 0)))
    st_t = jnp.transpose(st_prev, (0, 3, 1, 2)).reshape(b, SSM_STATE, SSM_INNER)
    mix, ca, cs, st = mixer_even(proj.reshape(b, l, PROJ_W), *mixer_w, ca8, cs8, st_t)
    x = rms_matmul(mix.reshape(b * l, -1), w_out, res=x, tm=tm, name="mixer_out_proj")
    st = jnp.transpose(st.reshape(b, SSM_STATE, SSM_HEADS, SSM_HEAD_DIM), (0, 2, 3, 1))
    return x, ca[:, 6:8], cs[:, 5:8], st


def _mem_kv_body(mem_ref, g_ref, wk_ref, wv_ref, kn_ref, k_ref, v_ref, kb_ref, vb_ref):
    mm = _rms(mem_ref[0], g_ref[...]).astype(BF16)
    k = jnp.dot(mm, wk_ref[...], preferred_element_type=F32)
    v = jnp.dot(mm, wv_ref[...], preferred_element_type=F32)
    for h in range(MEM_HEADS):
        sl = slice(h * MEM_HEAD_DIM, (h + 1) * MEM_HEAD_DIM)
        kh = _rms(k[:, sl], kn_ref[...])
        k_ref[0, :, sl] = kh
        kb_ref[0, :, sl] = kh.astype(BF16)
    v_ref[0] = v
    vb_ref[0] = v.astype(BF16)


def mem_kv(mem, src_norm, w_k, w_v, k_norm):
    B, Mt, D = mem.shape
    const = lambda shape: pl.BlockSpec(shape, lambda b: (0,) * len(shape))
    blk = pl.BlockSpec((1, Mt, D), lambda b: (b, 0, 0))
    return pl.pallas_call(
        _mem_kv_body,
        name="mem_kv",
        grid=(B,),
        in_specs=[blk, const((1, D)), const((D, D)), const((D, D)), const((1, MEM_HEAD_DIM))],
        out_specs=[blk, blk, blk, blk],
        out_shape=[jax.ShapeDtypeStruct((B, Mt, D), F32), jax.ShapeDtypeStruct((B, Mt, D), F32),
                   jax.ShapeDtypeStruct((B, Mt, D), BF16), jax.ShapeDtypeStruct((B, Mt, D), BF16)],
        compiler_params=_params("parallel"),
    )(mem, src_norm.reshape(1, D), w_k, w_v, k_norm.reshape(1, MEM_HEAD_DIM))


def _mem_attn_body(x_ref, g_ref, wq_ref, qn_ref, k_ref, v_ref, wo_ref, o_ref, *, nb):
    x = x_ref[...]
    tm = x.shape[0]
    rows = tm // nb
    hn = _rms(x, g_ref[...]).astype(BF16)
    q = jnp.dot(hn, wq_ref[...], preferred_element_type=F32)
    scale = MEM_HEAD_DIM ** -0.5
    outs = []
    for h in range(MEM_HEADS):
        sl = slice(h * MEM_HEAD_DIM, (h + 1) * MEM_HEAD_DIM)
        qh = _rms(q[:, sl], qn_ref[...]).astype(BF16)
        parts = []
        for s in range(nb):
            qs = qh[s * rows:(s + 1) * rows]
            sc = lax.dot_general(qs, k_ref[s, :, sl], NT_DIMS, preferred_element_type=F32) * scale
            sc = sc - jnp.max(sc, axis=-1, keepdims=True)
            p = jnp.exp(sc)
            p = p / jnp.sum(p, axis=-1, keepdims=True)
            parts.append(jnp.dot(p.astype(BF16), v_ref[s, :, sl], preferred_element_type=F32))
        outs.append(parts[0] if nb == 1 else jnp.concatenate(parts, axis=0))
    o = jnp.concatenate(outs, axis=1).astype(BF16)
    o_ref[...] = x + jnp.dot(o, wo_ref[...], preferred_element_type=F32)


def mem_attend(x, norm_g, k_b, v_b, w_q, q_norm, w_o, rows_per_seq, tm=512):
    M, D = x.shape
    tm = min(tm, M)
    nb = max(1, tm // rows_per_seq)
    tiles_per_seq = max(1, rows_per_seq // tm)
    const = lambda shape: pl.BlockSpec(shape, lambda i: (0,) * len(shape))
    kv_spec = pl.BlockSpec((nb, MEM_TOKENS, D), lambda i: (i // tiles_per_seq, 0, 0))
    return pl.pallas_call(
        functools.partial(_mem_attn_body, nb=nb),
        name="mem_attn",
        grid=(M // tm,),
        in_specs=[pl.BlockSpec((tm, D), lambda i: (i, 0)), const((1, D)), const((D, D)),
                  const((1, MEM_HEAD_DIM)), kv_spec, kv_spec, const((D, D))],
        out_specs=pl.BlockSpec((tm, D), lambda i: (i, 0)),
        out_shape=jax.ShapeDtypeStruct((M, D), F32),
        compiler_params=_params("parallel"),
    )(x, norm_g.reshape(1, D), w_q, q_norm.reshape(1, MEM_HEAD_DIM), k_b, v_b, w_o)


def _swiglu_step(xn, w1, w3, w2):
    h1 = jnp.dot(xn, w1, preferred_element_type=F32)
    h3 = jnp.dot(xn, w3, preferred_element_type=F32)
    a = (h1 * jax.nn.sigmoid(h1) * h3).astype(BF16)
    return jnp.dot(a, w2, preferred_element_type=F32)


def _ffn_dense_body(x_ref, g_ref, w1_ref, w3_ref, w2_ref, o_ref, xn_ref, acc_ref):
    f = pl.program_id(1)

    @pl.when(f == 0)
    def _():
        xn_ref[...] = _rms(x_ref[...], g_ref[...]).astype(BF16)
        acc_ref[...] = jnp.zeros_like(acc_ref)

    acc_ref[...] += _swiglu_step(xn_ref[...], w1_ref[...], w3_ref[...], w2_ref[...])

    @pl.when(f == pl.num_programs(1) - 1)
    def _():
        o_ref[...] = x_ref[...] + acc_ref[...]


def ffn_dense(x, norm_g, w1, w3, w2, tm=512, tf=None):
    M, D = x.shape
    F = w1.shape[1]
    tm = min(tm, M)
    tf = tf or F
    return pl.pallas_call(
        _ffn_dense_body,
        name="ffn_dense",
        grid=(M // tm, F // tf),
        in_specs=[pl.BlockSpec((tm, D), lambda i, f: (i, 0)), pl.BlockSpec((1, D), lambda i, f: (0, 0)),
                  pl.BlockSpec((D, tf), lambda i, f: (0, f)), pl.BlockSpec((D, tf), lambda i, f: (0, f)),
                  pl.BlockSpec((tf, D), lambda i, f: (f, 0))],
        out_specs=pl.BlockSpec((tm, D), lambda i, f: (i, 0)),
        out_shape=jax.ShapeDtypeStruct((M, D), F32),
        scratch_shapes=[pltpu.VMEM((tm, D), BF16), pltpu.VMEM((tm, D), F32)],
        compiler_params=_params("parallel", "arbitrary"),
    )(x, norm_g.reshape(1, D), w1, w3, w2)


def _ffn_expert_body(te_ref, tv_ref, x_ref, w1_ref, w3_ref, w2_ref, o_ref, xn_ref, acc_ref):
    i = pl.program_id(0)
    f = pl.program_id(1)
    last = f == pl.num_programs(1) - 1

    @pl.when(tv_ref[i] > 0)
    def _():
        @pl.when(f == 0)
        def _():
            xn_ref[...] = x_ref[...].astype(BF16)
            acc_ref[...] = jnp.zeros_like(acc_ref)

        acc_ref[...] += _swiglu_step(xn_ref[...], w1_ref[0], w3_ref[0], w2_ref[0])

        @pl.when(last)
        def _():
            o_ref[...] = acc_ref[...]

    @pl.when(jnp.logical_and(tv_ref[i] == 0, last))
    def _():
        o_ref[...] = jnp.zeros_like(o_ref)


def ffn_experts(xs, tile_expert, tile_valid, w1, w3, w2, tm, tf=1792):
    N, D = xs.shape
    F = w1.shape[2]
    grid_spec = pltpu.PrefetchScalarGridSpec(
        num_scalar_prefetch=2,
        grid=(N // tm, F // tf),
        in_specs=[pl.BlockSpec((tm, D), lambda i, f, te, tv: (i, 0)),
                  pl.BlockSpec((1, D, tf), lambda i, f, te, tv: (te[i], 0, f)),
                  pl.BlockSpec((1, D, tf), lambda i, f, te, tv: (te[i], 0, f)),
                  pl.BlockSpec((1, tf, D), lambda i, f, te, tv: (te[i], f, 0))],
        out_specs=pl.BlockSpec((tm, D), lambda i, f, te, tv: (i, 0)),
        scratch_shapes=[pltpu.VMEM((tm, D), BF16), pltpu.VMEM((tm, D), F32)],
    )
    return pl.pallas_call(
        _ffn_expert_body,
        name="ffn_experts",
        grid_spec=grid_spec,
        out_shape=jax.ShapeDtypeStruct((N, D), F32),
        compiler_params=_params("arbitrary", "arbitrary"),
    )(tile_expert, tile_valid, xs, w1, w3, w2)


def _router_body(x_ref, g_ref, wr_ref, route_ref):
    hn = _rms(x_ref[...], g_ref[...])
    xh = hn.astype(BF16)
    xm = (hn - xh.astype(F32)).astype(BF16)
    a = jnp.dot(xh, wr_ref[...], preferred_element_type=F32)
    b = jnp.dot(xm, wr_ref[...], preferred_element_type=F32)
    logits = (a[:, :128] + a[:, 128:]) + (b[:, :128] + b[:, 128:])
    lane = lax.broadcasted_iota(jnp.int32, logits.shape, 1)
    lane_f = lane.astype(F32)
    neg = -jnp.inf
    lg = jnp.where(lane < N_EXPERTS, logits, neg)
    m1 = jnp.max(lg, axis=-1, keepdims=True)
    i1 = jnp.min(jnp.where(lg == m1, lane_f, 128.0), axis=-1, keepdims=True)
    lg2 = jnp.where(lane_f == i1, neg, lg)
    m2 = jnp.max(lg2, axis=-1, keepdims=True)
    i2 = jnp.min(jnp.where(lg2 == m2, lane_f, 128.0), axis=-1, keepdims=True)
    e = jnp.exp(m2 - m1)
    den = 1.0 + e
    g1 = 1.0 / den
    g2 = e / den
    route = jnp.where(lane == 0, g1, jnp.where(lane == 1, g2, jnp.where(lane == 2, i1, jnp.where(lane == 3, i2, 0.0))))
    route_ref[...] = route


def router(x, norm_g, w_router, tm=512):
    M, D = x.shape
    tm = min(tm, M)
    wr = jnp.pad(w_router.astype(F32), ((0, 0), (0, 128 - N_EXPERTS)))
    wh = wr.astype(BF16)
    wm = (wr - wh.astype(F32)).astype(BF16)
    w2 = jnp.concatenate([wh, wm], axis=1)
    return pl.pallas_call(
        _router_body,
        name="router",
        grid=(M // tm,),
        in_specs=[pl.BlockSpec((tm, D), lambda i: (i, 0)), pl.BlockSpec((1, D), lambda i: (0, 0)),
                  pl.BlockSpec((D, 256), lambda i: (0, 0))],
        out_specs=pl.BlockSpec((tm, 128), lambda i: (i, 0)),
        out_shape=jax.ShapeDtypeStruct((M, 128), F32),
        compiler_params=_params("parallel"),
    )(x, norm_g.reshape(1, D), w2)


def _row_put(src_ref, r, dst_hbm, idx, sem):
    return pltpu.make_async_copy(src_ref.at[pl.ds(r, 1), :], dst_hbm.at[pl.ds(idx, 1), :], sem)


def _row_get(src_hbm, idx, dst_ref, r, sem):
    return pltpu.make_async_copy(src_hbm.at[pl.ds(idx, 1), :], dst_ref.at[pl.ds(r, 1), :], sem)


def _dispatch_body(i1_ref, i2_ref, x_ref, g_ref, xs_in_hbm, xs_hbm, hn_ref, sem):
    del xs_in_hbm
    tm = hn_ref.shape[0]
    hn_ref[...] = _rms(x_ref[...], g_ref[...])

    def start(r, _):
        _row_put(hn_ref, r, xs_hbm, i1_ref[0, 0, r], sem).start()
        _row_put(hn_ref, r, xs_hbm, i2_ref[0, 0, r], sem).start()
        return 0

    lax.fori_loop(0, tm, start, 0, unroll=8)

    def wait(r, _):
        _row_put(hn_ref, r, xs_hbm, 0, sem).wait()
        _row_put(hn_ref, r, xs_hbm, 0, sem).wait()
        return 0

    lax.fori_loop(0, tm, wait, 0, unroll=8)


def dispatch_rows(x, norm_g, pos1, pos2, xs, tm=512):
    M, D = x.shape
    tm = min(tm, M)
    idx_spec = pl.BlockSpec((1, 1, tm), lambda i: (i, 0, 0), memory_space=pltpu.SMEM)
    return pl.pallas_call(
        _dispatch_body,
        name="dispatch_rows",
        grid=(M // tm,),
        in_specs=[idx_spec, idx_spec, pl.BlockSpec((tm, D), lambda i: (i, 0)), pl.BlockSpec((1, D), lambda i: (0, 0)),
                  pl.BlockSpec(memory_space=pl.ANY)],
        out_specs=pl.BlockSpec(memory_space=pl.ANY),
        out_shape=jax.ShapeDtypeStruct(xs.shape, xs.dtype),
        input_output_aliases={4: 0},
        scratch_shapes=[pltpu.VMEM((tm, D), F32), pltpu.SemaphoreType.DMA(())],
        compiler_params=_params("arbitrary"),
    )(pos1.reshape(M // tm, 1, tm), pos2.reshape(M // tm, 1, tm), x, norm_g.reshape(1, D), xs)


def _combine_body(i1_ref, i2_ref, x_ref, route_ref, ys_hbm, o_ref, b1, b2, sem1, sem2):
    tm = o_ref.shape[0]

    def start(r, _):
        _row_get(ys_hbm, i1_ref[0, 0, r], b1, r, sem1).start()
        _row_get(ys_hbm, i2_ref[0, 0, r], b2, r, sem2).start()
        return 0

    lax.fori_loop(0, tm, start, 0, unroll=8)

    def wait(r, _):
        _row_get(ys_hbm, 0, b1, r, sem1).wait()
        _row_get(ys_hbm, 0, b2, r, sem2).wait()
        return 0

    lax.fori_loop(0, tm, wait, 0, unroll=8)
    o_ref[...] = x_ref[...] + route_ref[:, 0:1] * b1[...] + route_ref[:, 1:2] * b2[...]


def combine_rows(x, route, ys, pos1, pos2, tm=256):
    M, D = x.shape
    tm = min(tm, M)
    idx_spec = pl.BlockSpec((1, 1, tm), lambda i: (i, 0, 0), memory_space=pltpu.SMEM)
    return pl.pallas_call(
        _combine_body,
        name="combine_rows",
        grid=(M // tm,),
        in_specs=[idx_spec, idx_spec, pl.BlockSpec((tm, D), lambda i: (i, 0)),
                  pl.BlockSpec((tm, 128), lambda i: (i, 0)), pl.BlockSpec(memory_space=pl.ANY)],
        out_specs=pl.BlockSpec((tm, D), lambda i: (i, 0)),
        out_shape=jax.ShapeDtypeStruct((M, D), F32),
        scratch_shapes=[pltpu.VMEM((tm, D), F32), pltpu.VMEM((tm, D), F32),
                        pltpu.SemaphoreType.DMA(()), pltpu.SemaphoreType.DMA(())],
        compiler_params=_params("arbitrary"),
    )(pos1.reshape(M // tm, 1, tm), pos2.reshape(M // tm, 1, tm), x, route, ys)


def moe_ffn(xs_list, norm_g, w_router, w1, w3, w2, tm_e):
    D = xs_list[0].shape[1]
    routes = [router(x, norm_g, w_router) for x in xs_list]
    top_i = jnp.concatenate([route[:, 2:4] for route in routes], axis=0).astype(jnp.int32)
    T = top_i.shape[0]
    e_flat = top_i.reshape(-1)
    onehot = (e_flat[:, None] == jnp.arange(N_EXPERTS, dtype=jnp.int32)[None, :]).astype(jnp.int32)
    csum = jnp.cumsum(onehot, axis=0)
    counts = csum[-1]
    rank = jnp.take_along_axis(csum, e_flat[:, None], axis=1)[:, 0] - 1
    ptiles = (counts + tm_e - 1) // tm_e
    tile_end = jnp.cumsum(ptiles)
    pos = (tile_end - ptiles)[e_flat] * tm_e + rank
    pos1, pos2 = pos[0::2], pos[1::2]
    n_tiles = (2 * T) // tm_e + N_EXPERTS
    tile_ids = jnp.arange(n_tiles, dtype=jnp.int32)
    tile_valid = (tile_ids < tile_end[-1]).astype(jnp.int32)
    last_e = jnp.max(jnp.where(counts > 0, jnp.arange(N_EXPERTS, dtype=jnp.int32), 0))
    tile_expert = jnp.sum((tile_end[None, :] <= tile_ids[:, None]).astype(jnp.int32), axis=1)
    tile_expert = jnp.minimum(tile_expert, last_e)
    xs = jnp.zeros((n_tiles * tm_e, D), F32)
    t0 = 0
    for x in xs_list:
        t1 = t0 + x.shape[0]
        xs = dispatch_rows(x, norm_g, pos1[t0:t1], pos2[t0:t1], xs)
        t0 = t1
    ys = ffn_experts(xs, tile_expert, tile_valid, w1, w3, w2, tm_e)
    outs = []
    t0 = 0
    for x, route in zip(xs_list, routes):
        t1 = t0 + x.shape[0]
        outs.append(combine_rows(x, route, ys, pos1[t0:t1], pos2[t0:t1]))
        t0 = t1
    return outs


def _rope_lanes(v, cos, sin, lane):
    partner = jnp.where(lane < 80, pltpu.roll(v, 112, 1), pltpu.roll(v, 16, 1))
    return v * cos + partner * sin


def _mla_rows_body(x_ref, g_ref, wd_ref, qln_ref, kvln_ref, krn_ref, wuq_ref, qn_ref, cos_ref, sin_ref,
                   q_ref, ckv_ref, kr_ref):
    hn = _rms(x_ref[...], g_ref[...]).astype(BF16)
    d = jnp.dot(hn, wd_ref[...], preferred_element_type=F32)
    cq = _rms(d[:, 0:MLA_Q_LORA], qln_ref[...]).astype(BF16)
    ckv_ref[...] = _rms(d[:, MLA_Q_LORA:MLA_Q_LORA + MLA_KV_LORA], kvln_ref[...])
    cos = cos_ref[...]
    sin = sin_ref[...]
    lane = lax.broadcasted_iota(jnp.int32, cos.shape, 1)
    kr = d[:, MLA_Q_LORA + MLA_KV_LORA:]
    kr = kr * lax.rsqrt(jnp.sum(kr * kr, axis=-1, keepdims=True) * (1.0 / MLA_ROPE) + EPS) * krn_ref[...]
    kr_ref[...] = _rope_lanes(kr, cos, sin, lane)
    q = jnp.dot(cq, wuq_ref[...], preferred_element_type=F32)
    nope = lane < MLA_NOPE
    for h in range(MLA_HEADS):
        qh = q[:, h * 128:(h + 1) * 128]
        sq = qh * qh
        ss_n = jnp.sum(jnp.where(nope, sq, 0.0), axis=-1, keepdims=True) * (1.0 / MLA_NOPE)
        ss_r = jnp.sum(jnp.where(nope, 0.0, sq), axis=-1, keepdims=True) * (1.0 / MLA_ROPE)
        r = jnp.where(nope, lax.rsqrt(ss_n + EPS), lax.rsqrt(ss_r + EPS))
        qh = _rope_lanes(qh * r * qn_ref[...], cos, sin, lane)
        q_ref[:, h * 128:(h + 1) * 128] = (qh * (MLA_SCALE * math.log2(math.e))).astype(BF16)


def mla_rows(x, norm_g, wd_pad, q_lora_norm, kv_lora_norm, krn_pad, wuq_pad, qn_pad, cos_t, sin_t, tm=512):
    M, D = x.shape
    tm = min(tm, M)
    const = lambda shape: pl.BlockSpec(shape, lambda i: (0,) * len(shape))
    rows = lambda n: pl.BlockSpec((tm, n), lambda i: (i, 0))
    table_tiles = cos_t.shape[0] // tm
    table = pl.BlockSpec((tm, 128), lambda i: (i % table_tiles, 0))
    return pl.pallas_call(
        _mla_rows_body,
        name="mla_rows",
        grid=(M // tm,),
        in_specs=[rows(D), const((1, D)), const(wd_pad.shape), const((1, MLA_Q_LORA)), const((1, MLA_KV_LORA)),
                  const((1, 128)), const(wuq_pad.shape), const((1, 128)), table, table],
        out_specs=[rows(MLA_HEADS * 128), rows(MLA_KV_LORA), rows(128)],
        out_shape=[jax.ShapeDtypeStruct((M, MLA_HEADS * 128), BF16), jax.ShapeDtypeStruct((M, MLA_KV_LORA), F32),
                   jax.ShapeDtypeStruct((M, 128), F32)],
        compiler_params=_params("parallel"),
    )(x, norm_g.reshape(1, D), wd_pad, q_lora_norm.reshape(1, -1), kv_lora_norm.reshape(1, -1), krn_pad,
      wuq_pad, qn_pad, cos_t, sin_t)


def _mla_kv_body(ckv_ref, kr_ref, wuk_ref, wuv_ref, kn_ref, k_ref, v_ref):
    c = ckv_ref[...].astype(BF16)
    kn = jnp.dot(c, wuk_ref[...], preferred_element_type=F32)
    lane = lax.broadcasted_iota(jnp.int32, (1, MLA_HEADS * 128), 1)
    ones_col = jnp.where(jnp.bitwise_and(lane, 127) == MLA_V, 1.0, 0.0)
    v_ref[...] = (jnp.dot(c, wuv_ref[...], preferred_element_type=F32) + ones_col).astype(BF16)
    kr = kr_ref[...]
    for h in range(MLA_HEADS):
        kh = kn[:, h * 128:(h + 1) * 128]
        r = lax.rsqrt(jnp.sum(kh * kh, axis=-1, keepdims=True) * (1.0 / MLA_NOPE) + EPS)
        k_ref[:, h * 128:(h + 1) * 128] = (kh * r * kn_ref[...] + kr).astype(BF16)


def mla_kv(ckv, kr_pad, wuk_pad, wuv, kn_pad, tm=512):
    M = ckv.shape[0]
    tm = min(tm, M)
    const = lambda shape: pl.BlockSpec(shape, lambda i: (0,) * len(shape))
    rows = lambda n: pl.BlockSpec((tm, n), lambda i: (i, 0))
    return pl.pallas_call(
        _mla_kv_body,
        name="mla_kv",
        grid=(M // tm,),
        in_specs=[rows(MLA_KV_LORA), rows(128), const(wuk_pad.shape), const(wuv.shape), const((1, 128))],
        out_specs=[rows(MLA_HEADS * 128), rows(MLA_HEADS * 128)],
        out_shape=[jax.ShapeDtypeStruct((M, MLA_HEADS * 128), BF16),
                   jax.ShapeDtypeStruct((M, MLA_HEADS * 128), BF16)],
        compiler_params=_params("parallel"),
    )(ckv, kr_pad, wuk_pad, wuv, kn_pad)


def _flash_body(qi_ref, ki_ref, q_ref, k_ref, v_ref, *rest, causal):
    if causal:
        bias_ref, o_ref, m_ref, acc_ref = rest
    else:
        o_ref, m_ref, acc_ref = rest
    t = pl.program_id(2)
    qi = qi_ref[t]
    ki = ki_ref[t]
    tq = q_ref.shape[1]

    @pl.when(ki == 0)
    def _():
        m_ref[...] = jnp.full_like(m_ref, -jnp.inf)
        acc_ref[...] = jnp.zeros_like(acc_ref)

    def step(masked):
        tk = k_ref.shape[1]
        sub = min(FLASH_KV_SUB, tk) if masked else tk
        for h in range(2):
            hl = slice(h * 128, (h + 1) * 128)
            m, acc = m_ref[h], acc_ref[h]
            for j in range(tk // sub):
                r0 = j * sub if masked else 0
                cols = slice(j * sub, (j + 1) * sub)
                s = lax.dot_general(q_ref[0, r0:, hl], k_ref[0, cols, hl], NT_DIMS, preferred_element_type=F32)
                if masked:
                    s = s + bias_ref[r0:, cols]
                m_new = jnp.maximum(m[r0:], jnp.broadcast_to(jnp.max(s, axis=-1, keepdims=True), (tq - r0, 128)))
                alpha = jnp.exp2(m[r0:] - m_new)
                m_wide = jnp.concatenate([m_new] * (sub // 128), axis=1) if sub % 128 == 0 else m_new[:, 0:1]
                p = jnp.exp2(s - m_wide)
                acc_new = alpha * acc[r0:] + jnp.dot(p.astype(BF16), v_ref[0, cols, hl], preferred_element_type=F32)
                if r0:
                    m_new = jnp.concatenate([m[:r0], m_new], axis=0)
                    acc_new = jnp.concatenate([acc[:r0], acc_new], axis=0)
                m, acc = m_new, acc_new
            m_ref[h], acc_ref[h] = m, acc

    def finish():
        lane = lax.broadcasted_iota(jnp.int32, (tq, 128), 1)
        o0 = acc_ref[0] / acc_ref[0, :, MLA_V:MLA_V + 1]
        o1 = acc_ref[1] / acc_ref[1, :, MLA_V:MLA_V + 1]
        o_ref[0] = jnp.where(lane < MLA_V, o0, pltpu.roll(o1, MLA_V, 1)).astype(o_ref.dtype)

    if causal:
        @pl.when(ki < qi)
        def _():
            step(False)

        @pl.when(ki == qi)
        def _():
            step(True)
            finish()
    else:
        step(False)

        @pl.when(t == pl.num_programs(2) - 1)
        def _():
            finish()


def mla_attention(q, k, v, causal, tq, tk):
    B, Lq, _ = q.shape
    Lk = k.shape[1]
    nq, nk = Lq // tq, Lk // tk
    pairs = [(i, j) for i in range(nq) for j in range(nk) if (j <= i or not causal)]
    qi_tab = jnp.asarray([i for i, _ in pairs], jnp.int32)
    ki_tab = jnp.asarray([j for _, j in pairs], jnp.int32)
    in_specs = [pl.BlockSpec((1, tq, 256), lambda b, p, t, qt, kt: (b, qt[t], p)),
                pl.BlockSpec((1, tk, 256), lambda b, p, t, qt, kt: (b, kt[t], p)),
                pl.BlockSpec((1, tk, 256), lambda b, p, t, qt, kt: (b, kt[t], p))]
    args = [q, k, v]
    if causal:
        chunk = jnp.arange(tq) // CHUNK
        bias = jnp.where(chunk[None, :] <= chunk[:, None], 0.0, -jnp.inf).astype(F32)
        in_specs.append(pl.BlockSpec((tq, tk), lambda b, p, t, qt, kt: (0, 0)))
        args.append(bias)
    grid_spec = pltpu.PrefetchScalarGridSpec(
        num_scalar_prefetch=2,
        grid=(B, MLA_HEADS // 2, len(pairs)),
        in_specs=in_specs,
        out_specs=pl.BlockSpec((1, tq, 128), lambda b, p, t, qt, kt: (b, qt[t], p)),
        scratch_shapes=[pltpu.VMEM((2, tq, 128), F32), pltpu.VMEM((2, tq, 128), F32)],
    )
    return pl.pallas_call(
        functools.partial(_flash_body, causal=causal),
        name="mla_flash",
        grid_spec=grid_spec,
        out_shape=jax.ShapeDtypeStruct((B, Lq, MLA_HEADS * MLA_V), BF16),
        compiler_params=_params("parallel", "parallel", "arbitrary"),
    )(qi_tab, ki_tab, *args)


def _rope_tables(pos):
    half = MLA_ROPE // 2
    inv = ROPE_THETA ** (-jnp.arange(half, dtype=F32) / half)
    ang = pos.astype(F32)[:, None] * inv[None, :]
    cos, sin = jnp.cos(ang), jnp.sin(ang)
    n = pos.shape[0]
    ones = jnp.ones((n, MLA_NOPE), F32)
    zeros = jnp.zeros((n, 128 - MLA_NOPE - MLA_ROPE), F32)
    cos_t = jnp.concatenate([ones, cos, cos, zeros], axis=1)
    sin_t = jnp.concatenate([jnp.zeros((n, MLA_NOPE), F32), -sin, sin, zeros], axis=1)
    return cos_t, sin_t


def _pad_heads(w, d_in, d_head, d_pad):
    w = w.reshape(d_in, MLA_HEADS, d_head)
    return jnp.pad(w, ((0, 0), (0, 0), (0, d_pad - d_head))).reshape(d_in, MLA_HEADS * d_pad)


def mla_weights(norm_g, w_down, q_lora_norm, kv_lora_norm, w_uq, w_ukv, q_nope_norm, q_rope_norm, k_nope_norm,
                k_rope_norm, w_o):
    D = w_down.shape[0]
    split = MLA_Q_LORA + MLA_KV_LORA
    tail = 128 - MLA_NOPE - MLA_ROPE
    wd_pad = jnp.concatenate([w_down[:, :split], jnp.zeros((D, MLA_NOPE), F32), w_down[:, split:],
                              jnp.zeros((D, tail), F32)], axis=1).astype(BF16)
    lane_pad = lambda a, b: jnp.concatenate([a, b, jnp.zeros((128 - a.shape[0] - b.shape[0],), F32)]).reshape(1, 128)
    krn_pad = lane_pad(jnp.zeros((MLA_NOPE,), F32), k_rope_norm)
    qn_pad = lane_pad(q_nope_norm, q_rope_norm)
    kn_pad = lane_pad(k_nope_norm, jnp.zeros((0,), F32))
    wuq_pad = _pad_heads(w_uq, MLA_Q_LORA, MLA_NOPE + MLA_ROPE, 128).astype(BF16)
    wukv = w_ukv.reshape(MLA_KV_LORA, MLA_HEADS, MLA_NOPE + MLA_V)
    wuk_pad = _pad_heads(wukv[:, :, :MLA_NOPE].reshape(MLA_KV_LORA, -1), MLA_KV_LORA, MLA_NOPE, 128).astype(BF16)
    wuv = _pad_heads(wukv[:, :, MLA_NOPE:].reshape(MLA_KV_LORA, -1), MLA_KV_LORA, MLA_V, 128).astype(BF16)
    rows_w = (norm_g, wd_pad, q_lora_norm, kv_lora_norm, krn_pad, wuq_pad, qn_pad)
    return rows_w, (wuk_pad, wuv, kn_pad), w_o.astype(BF16)


def mla_block(x, b, l, mw, ckv_past, kr_past, tile):
    rows_w, kv_w, w_o = mw
    n_past = 0 if ckv_past is None else ckv_past.shape[1]
    cos_t, sin_t = _rope_tables(n_past + jnp.arange(l))
    reps = max(1, min(b * l, 512) // l)
    q, ckv, kr = mla_rows(x, *rows_w, jnp.tile(cos_t, (reps, 1)), jnp.tile(sin_t, (reps, 1)))
    kr_out = kr[:, MLA_NOPE:MLA_NOPE + MLA_ROPE]
    if ckv_past is None:
        k_all, v_all = mla_kv(ckv, kr, *kv_w)
        o = mla_attention(q.reshape(b, l, -1), k_all.reshape(b, l, -1), v_all.reshape(b, l, -1), True, tile, tile)
    else:
        lk = n_past + l
        ckv_all = jnp.concatenate([ckv_past, ckv.reshape(b, l, -1)], axis=1).reshape(b * lk, -1)
        kr_pad = jnp.pad(kr_past, ((0, 0), (0, 0), (MLA_NOPE, 128 - MLA_NOPE - MLA_ROPE)))
        kr_all = jnp.concatenate([kr_pad, kr.reshape(b, l, 128)], axis=1).reshape(b * lk, 128)
        k_all, v_all = mla_kv(ckv_all, kr_all, *kv_w, tm=lk // 5)
        o = mla_attention(q.reshape(b, l, -1), k_all.reshape(b, lk, -1), v_all.reshape(b, lk, -1), False, l, lk)
    x = rms_matmul(o.reshape(b * l, -1), w_o, res=x, tm=1024, name="mla_out_proj")
    return x, ckv, kr_out


def kernel(x_prompt, x_sample, mem_prompt, state_conv_a, state_conv_ssm, state_ssm, cache_mla_ckv, cache_mla_krope, cache_mem_k, cache_mem_v, norm_mix, norm_mem, norm_ffn, w_in_e, conv_a_w, conv_s_w, conv_s_b, dt_bias, a_log, d_skip, ssm_norm, w_out_e, w_down_o, q_lora_norm, kv_lora_norm, w_uq, w_ukv, q_nope_norm, q_rope_norm, k_nope_norm, k_rope_norm, w_o_mla, mem_src_norm, w_mem_q, w_mem_k, w_mem_v, mem_q_norm, mem_k_norm, w_mem_o, w_ffn1, w_ffn3, w_ffn2, w_router, w_exp1, w_exp3, w_exp2):
    bp, lp, D = x_prompt.shape
    bs, ls, _ = x_sample.shape
    past = cache_mla_ckv.shape[2]
    bf = lambda w: w.astype(BF16)
    xp = x_prompt.reshape(bp * lp, D)
    xs = x_sample.reshape(bs * ls, D)

    ew = even_weights(w_in_e[0], w_out_e[0], norm_mix[0], conv_a_w[0], conv_s_w[0], conv_s_b[0], dt_bias[0],
                      a_log[0], d_skip[0], ssm_norm[0])
    xp, p_ca, p_cs, p_st = even_mixer_block(
        xp, bp, lp, ew, jnp.zeros((bp, 2, A_WIDTH), F32), jnp.zeros((bp, 3, SSM_XBC), F32),
        jnp.zeros((bp, SSM_HEADS, SSM_HEAD_DIM, SSM_STATE), F32), 1024)
    xs, s_ca, s_cs, s_st = even_mixer_block(xs, bs, ls, ew, state_conv_a[0], state_conv_ssm[0], state_ssm[0], 512)

    p_mk, p_mv = [], []

    def memory_block(i, xp, xs):
        mk, mv, mk_b, mv_b = mem_kv(mem_prompt, mem_src_norm[i], bf(w_mem_k[i]), bf(w_mem_v[i]), mem_k_norm[i])
        p_mk.append(mk.reshape(bp, MEM_TOKENS, MEM_HEADS, MEM_HEAD_DIM))
        p_mv.append(mv.reshape(bp, MEM_TOKENS, MEM_HEADS, MEM_HEAD_DIM))
        wq, wo = bf(w_mem_q[i]), bf(w_mem_o[i])
        xp = mem_attend(xp, norm_mem[i], mk_b, mv_b, wq, mem_q_norm[i], wo, lp)
        ck = bf(cache_mem_k[i]).reshape(bs, MEM_TOKENS, D)
        cv = bf(cache_mem_v[i]).reshape(bs, MEM_TOKENS, D)
        xs = mem_attend(xs, norm_mem[i], ck, cv, wq, mem_q_norm[i], wo, ls)
        return xp, xs

    xp, xs = memory_block(0, xp, xs)
    w1, w3, w2 = bf(w_ffn1[0]), bf(w_ffn3[0]), bf(w_ffn2[0])
    xp = ffn_dense(xp, norm_ffn[0], w1, w3, w2)
    xs = ffn_dense(xs, norm_ffn[0], w1, w3, w2)

    mw = mla_weights(norm_mix[1], w_down_o[0], q_lora_norm[0], kv_lora_norm[0], w_uq[0], w_ukv[0], q_nope_norm[0],
                     q_rope_norm[0], k_nope_norm[0], k_rope_norm[0], w_o_mla[0])
    xp, p_ckv, p_kr = mla_block(xp, bp, lp, mw, None, None, min(lp, 1024))
    xs, s_ckv, s_kr = mla_block(xs, bs, ls, mw, cache_mla_ckv[0], cache_mla_krope[0], None)

    xp, xs = memory_block(1, xp, xs)
    we1, we3, we2 = bf(w_exp1[0]), bf(w_exp3[0]), bf(w_exp2[0])
    xp, xs = moe_ffn([xp, xs], norm_ffn[1], w_router[0], we1, we3, we2, 512)

    return (xp.reshape(bp, lp, D), xs.reshape(bs, ls, D),
            p_ca[None], p_cs[None], p_st[None],
            p_ckv.reshape(1, bp, lp, -1), p_kr.reshape(1, bp, lp, -1), jnp.stack(p_mk), jnp.stack(p_mv),
            s_ca[None], s_cs[None], s_st[None],
            s_ckv.reshape(1, bs, ls, -1), s_kr.reshape(1, bs, ls, -1))
```

```python
import functools
import math

import jax
import jax.numpy as jnp
from jax import lax
from jax.experimental import pallas as pl
from jax.experimental.pallas import tpu as pltpu

F32 = jnp.float32
BF16 = jnp.bfloat16
EPS = 1e-6

D_MODEL = 1024
CHUNK = 64
A_WIDTH = 1024
SSM_HEADS = 32
SSM_HEAD_DIM = 64
SSM_INNER = 2048
SSM_STATE = 128
SSM_GROUPS = 4
SSM_XBC = 3072
HEAD_LANES = 128
PROJ_W = 3 * A_WIDTH + SSM_INNER + SSM_XBC + HEAD_LANES
COL_Z = 3 * A_WIDTH
COL_XBC = COL_Z + SSM_INNER
COL_DT = COL_XBC + SSM_XBC
MLA_HEADS = 16
MLA_Q_LORA = 512
MLA_KV_LORA = 256
MLA_NOPE = 64
MLA_ROPE = 32
MLA_V = 64
MLA_SCALE = (MLA_NOPE + MLA_ROPE) ** -0.5
ROPE_THETA = 10000.0
MEM_TOKENS = 256
MEM_HEADS = 4
MEM_HEAD_DIM = 256
N_EXPERTS = 8
VMEM_LIMIT = 56 * 1024 * 1024
MIXER_CHUNKS_PER_STEP = 4
FLASH_HEADS = 2
FLASH_KV_SUB = 512
NT_DIMS = (((1,), (1,)), ((), ()))


def _params(*sem):
    return pltpu.CompilerParams(dimension_semantics=sem, vmem_limit_bytes=VMEM_LIMIT)


def _rms(x, g):
    return x * lax.rsqrt(jnp.mean(x * x, axis=-1, keepdims=True) + EPS) * g


def _split3(x):
    hi = x.astype(BF16)
    r1 = x - hi.astype(F32)
    mid = r1.astype(BF16)
    lo = (r1 - mid.astype(F32)).astype(BF16)
    return hi, mid, lo


def _dot_sel(x, sel_bf16):
    hi, mid, lo = _split3(x)
    d = lambda a: jnp.dot(a, sel_bf16, preferred_element_type=F32)
    return d(hi) + d(mid) + d(lo)


def _dot_sel2(x, sel_bf16):
    hi = x.astype(BF16)
    mid = (x - hi.astype(F32)).astype(BF16)
    d = lambda a: jnp.dot(a, sel_bf16, preferred_element_type=F32)
    return d(hi) + d(mid)


def _sel_dot_nt(sel_bf16, x):
    hi, mid, lo = _split3(x)
    d = lambda a: lax.dot_general(sel_bf16, a, NT_DIMS, preferred_element_type=F32)
    return d(hi) + d(mid) + d(lo)


def _sel_dot(sel_bf16, x):
    hi, mid, lo = _split3(x)
    d = lambda a: jnp.dot(sel_bf16, a, preferred_element_type=F32)
    return d(hi) + d(mid) + d(lo)


def _rms_matmul_body(*refs, use_norm, use_res):
    it = iter(refs)
    x_ref = next(it)
    g_ref = next(it) if use_norm else None
    w_ref = next(it)
    r_ref = next(it) if use_res else None
    o_ref = next(it)
    xn_ref = next(it) if use_norm else None
    if use_norm:
        @pl.when(pl.program_id(1) == 0)
        def _():
            xn_ref[...] = _rms(x_ref[...], g_ref[...]).astype(BF16)
        xb = xn_ref[...]
    else:
        xb = x_ref[...]
    acc = jnp.dot(xb, w_ref[...], preferred_element_type=F32)
    if use_res:
        acc = acc + r_ref[...]
    o_ref[...] = acc.astype(o_ref.dtype)


def rms_matmul(x, w, g=None, res=None, out_dtype=F32, tm=512, tn=None, name="rms_matmul"):
    M, K = x.shape
    N = w.shape[1]
    tn = tn or N
    tm = min(tm, M)
    in_specs = [pl.BlockSpec((tm, K), lambda i, j: (i, 0))]
    args = [x]
    if g is not None:
        in_specs.append(pl.BlockSpec((1, K), lambda i, j: (0, 0)))
        args.append(g.reshape(1, K))
    in_specs.append(pl.BlockSpec((K, tn), lambda i, j: (0, j)))
    args.append(w)
    if res is not None:
        in_specs.append(pl.BlockSpec((tm, tn), lambda i, j: (i, j)))
        args.append(res)
    scratch = [pltpu.VMEM((tm, K), BF16)] if g is not None else []
    return pl.pallas_call(
        functools.partial(_rms_matmul_body, use_norm=g is not None, use_res=res is not None),
        name=name,
        grid=(M // tm, N // tn),
        in_specs=in_specs,
        out_specs=pl.BlockSpec((tm, tn), lambda i, j: (i, j)),
        out_shape=jax.ShapeDtypeStruct((M, N), out_dtype),
        scratch_shapes=scratch,
        compiler_params=_params("parallel", "arbitrary"),
    )(*args)


def _softplus(v):
    return jnp.maximum(v, 0.0) + jnp.log1p(jnp.exp(-jnp.abs(v)))


def _mixer_body(proj_ref, wa_ref, ws_ref, bs_ref, dtb_ref, alog_ref, dskip_ref, gn_ref,
                tri_ref, selrow_ref, expand_ref, eye_ref,
                ca_prev_ref, cs_prev_ref, st_prev_ref,
                mix_ref, ca_out_ref, cs_out_ref, st_ref,
                abuf, sbuf):
    c = pl.program_id(1)
    Q = CHUNK
    R = proj_ref.shape[1]

    @pl.when(c == 0)
    def _():
        abuf[0:8, :] = ca_prev_ref[0]
        sbuf[0:8, :] = cs_prev_ref[0]
        st_ref[0] = st_prev_ref[0]

    g_b = proj_ref[0, :, 0:A_WIDTH]
    abuf[8:8 + R, :] = proj_ref[0, :, A_WIDTH:2 * A_WIDTH] * proj_ref[0, :, 2 * A_WIDTH:3 * A_WIDTH]
    conv_a = wa_ref[0:1, :] * abuf[6:6 + R, :]
    conv_a = conv_a + wa_ref[1:2, :] * abuf[7:7 + R, :]
    conv_a = conv_a + wa_ref[2:3, :] * abuf[8:8 + R, :]
    mix_ref[0, :, 0:A_WIDTH] = (g_b * conv_a).astype(BF16)
    tail_a = abuf[R:R + 8, :]
    ca_out_ref[0] = tail_a
    abuf[0:8, :] = tail_a

    sbuf[8:8 + R, :] = proj_ref[0, :, COL_XBC:COL_DT]
    xc_all = ws_ref[0:1, :] * sbuf[5:5 + R, :]
    xc_all = xc_all + ws_ref[1:2, :] * sbuf[6:6 + R, :]
    xc_all = xc_all + ws_ref[2:3, :] * sbuf[7:7 + R, :]
    xc_all = xc_all + ws_ref[3:4, :] * sbuf[8:8 + R, :]
    xc_all = xc_all + bs_ref[...]
    xc_all = xc_all * jax.nn.sigmoid(xc_all)
    tail_s = sbuf[R:R + 8, :]
    cs_out_ref[0] = tail_s
    sbuf[0:8, :] = tail_s

    dt_all = _softplus(proj_ref[0, :, COL_DT:COL_DT + HEAD_LANES] + dtb_ref[...])
    la_all = dt_all * (-jnp.exp(alog_ref[...]))

    lane = lax.broadcasted_iota(jnp.int32, (Q, 128), 1)
    row = lax.broadcasted_iota(jnp.int32, (Q, 128), 0)
    tril2 = jnp.where(lane >= Q, lane - Q, lane) <= row
    left = lane < Q
    eye = eye_ref[...]

    for ci in range(R // Q):
        _ssd_chunk(slice(ci * Q, (ci + 1) * Q), xc_all, dt_all, la_all, tril2, left, eye,
                   proj_ref, dskip_ref, gn_ref, tri_ref, selrow_ref, expand_ref, mix_ref, st_ref)


def _ssd_chunk(rows, xc_all, dt_all, la_all, tril2, left, eye,
               proj_ref, dskip_ref, gn_ref, tri_ref, selrow_ref, expand_ref, mix_ref, st_ref):
    Q = CHUNK
    xc = xc_all[rows]
    dt = dt_all[rows]
    cs = _sel_dot(tri_ref[...], la_all[rows])
    cs_last = cs[Q - 1:Q, :]
    w_st = dt * jnp.exp(cs_last - cs)
    cs_x = _dot_sel(cs, expand_ref[...])
    w_st_x = _dot_sel2(w_st, expand_ref[...])
    e_in_x = jnp.exp(cs_x)
    dec_x = e_in_x[Q - 1:Q, :]
    cs_r = _sel_dot_nt(selrow_ref[...], cs)
    dt_r = _sel_dot_nt(selrow_ref[...], dt)
    cs_rp = jnp.concatenate([cs_r[0:16], cs_r[16:32]], axis=1)
    dt_rp = jnp.concatenate([dt_r[0:16], dt_r[16:32]], axis=1)

    for g in range(SSM_GROUPS):
        gl = slice(SSM_INNER + g * SSM_STATE, SSM_INNER + (g + 1) * SSM_STATE)
        Bg = xc[:, gl].astype(BF16)
        Cg = xc[:, gl.start + SSM_GROUPS * SSM_STATE: gl.stop + SSM_GROUPS * SSM_STATE].astype(BF16)
        sc = lax.dot_general(Cg, Bg, NT_DIMS, preferred_element_type=F32)
        sc2 = jnp.concatenate([sc, sc], axis=1)
        BgT = lax.dot_general(eye, Bg, NT_DIMS, preferred_element_type=F32).astype(BF16)
        hs = slice(g * 512, (g + 1) * 512)
        st_g = st_ref[0, :, hs]
        y_off = jnp.dot(Cg, st_g.astype(BF16), preferred_element_type=F32)
        xw = (xc[:, hs] * w_st_x[:, hs]).astype(BF16)
        st_ref[0, :, hs] = dec_x[:, hs] * st_g + jnp.dot(BgT, xw, preferred_element_type=F32)
        ys = []
        for kk in range(4):
            k = 4 * g + kk
            pl_ = slice(k * 128, (k + 1) * 128)
            diff = cs_x[:, pl_] - cs_rp[k:k + 1, :]
            decay = jnp.exp(jnp.where(tril2, diff, -jnp.inf))
            m_pair = (sc2 * decay * dt_rp[k:k + 1, :]).astype(BF16)
            xp = xc[:, pl_]
            rhs = jnp.concatenate([jnp.where(left, xp, 0.0), jnp.where(left, 0.0, xp)], axis=0).astype(BF16)
            y_diag = jnp.dot(m_pair, rhs, preferred_element_type=F32)
            y = y_diag + y_off[:, kk * 128:(kk + 1) * 128] * e_in_x[:, pl_]
            y = y + dskip_ref[:, pl_] * xp
            z = proj_ref[0, rows, COL_Z + k * 128: COL_Z + (k + 1) * 128]
            ys.append(y * (z * jax.nn.sigmoid(z)))
        yg = jnp.concatenate(ys, axis=1)
        yn = yg * lax.rsqrt(jnp.mean(yg * yg, axis=-1, keepdims=True) + EPS) * gn_ref[:, hs]
        mix_ref[0, rows, A_WIDTH + g * 512: A_WIDTH + (g + 1) * 512] = yn.astype(BF16)


def mixer_even(proj, conv_a_w, conv_s_w, conv_s_b, dt_bias, a_log, d_skip, ssm_norm,
               conv_a_prev8, conv_s_prev8, ssm_prev_t):
    B, L, _ = proj.shape
    rows = CHUNK * MIXER_CHUNKS_PER_STEP if L % (CHUNK * MIXER_CHUNKS_PER_STEP) == 0 else CHUNK
    pad_h =lambda v: jnp.pad(v.reshape(1, SSM_HEADS), ((0, 0), (0, HEAD_LANES - SSM_HEADS)))
    tri = (jnp.arange(CHUNK)[:, None] >= jnp.arange(CHUNK)[None, :]).astype(BF16)
    heads = jnp.arange(HEAD_LANES)
    order = jnp.concatenate([jnp.arange(0, SSM_HEADS, 2), jnp.arange(1, SSM_HEADS, 2)])
    selrow = (order[:, None] == heads[None, :]).astype(BF16)
    expand = (heads[:, None] == (jnp.arange(SSM_INNER) // SSM_HEAD_DIM)[None, :]).astype(BF16)
    eye = jnp.eye(SSM_STATE, dtype=BF16)
    dskip_x = jnp.repeat(d_skip.astype(F32), SSM_HEAD_DIM).reshape(1, SSM_INNER)
    const = lambda shape: pl.BlockSpec(shape, lambda b, c: (0,) * len(shape))
    per_b = lambda shape: pl.BlockSpec((1,) + shape, lambda b, c: (b,) + (0,) * len(shape))
    return pl.pallas_call(
        _mixer_body,
        name="ssd_mixer",
        grid=(B, L // rows),
        in_specs=[
            pl.BlockSpec((1, rows, PROJ_W), lambda b, c: (b, c, 0)),
            const((3, A_WIDTH)), const((4, SSM_XBC)), const((1, SSM_XBC)),
            const((1, HEAD_LANES)), const((1, HEAD_LANES)), const((1, SSM_INNER)), const((1, SSM_INNER)),
            const((CHUNK, CHUNK)), const((SSM_HEADS, HEAD_LANES)), const((HEAD_LANES, SSM_INNER)),
            const((SSM_STATE, SSM_STATE)),
            per_b((8, A_WIDTH)), per_b((8, SSM_XBC)), per_b((SSM_STATE, SSM_INNER)),
        ],
        out_specs=[
            pl.BlockSpec((1, rows, A_WIDTH + SSM_INNER), lambda b, c: (b, c, 0)),
            per_b((8, A_WIDTH)), per_b((8, SSM_XBC)), per_b((SSM_STATE, SSM_INNER)),
        ],
        out_shape=[
            jax.ShapeDtypeStruct((B, L, A_WIDTH + SSM_INNER), BF16),
            jax.ShapeDtypeStruct((B, 8, A_WIDTH), F32),
            jax.ShapeDtypeStruct((B, 8, SSM_XBC), F32),
            jax.ShapeDtypeStruct((B, SSM_STATE, SSM_INNER), F32),
        ],
        scratch_shapes=[
            pltpu.VMEM((rows + 8, A_WIDTH), F32),
            pltpu.VMEM((rows + 8, SSM_XBC), F32),
        ],
        compiler_params=_params("parallel", "arbitrary"),
    )(proj, conv_a_w, conv_s_w, conv_s_b.reshape(1, SSM_XBC), pad_h(dt_bias), pad_h(a_log), dskip_x,
      ssm_norm.reshape(1, SSM_INNER), tri, selrow, expand, eye, conv_a_prev8, conv_s_prev8, ssm_prev_t)


def even_weights(w_in, w_out, norm_g, conv_a_w, conv_s_w, conv_s_b, dt_bias, a_log, d_skip, ssm_norm):
    w_in_pad = jnp.pad(w_in, ((0, 0), (0, PROJ_W - w_in.shape[1]))).astype(BF16)
    return (norm_g, w_in_pad, w_out.astype(BF16), (conv_a_w, conv_s_w, conv_s_b, dt_bias, a_log, d_skip, ssm_norm))


def even_mixer_block(x, b, l, ew, ca_prev, cs_prev, st_prev, tm):
    norm_g, w_in_pad, w_out, mixer_w = ew
    proj = rms_matmul(x, w_in_pad, g=norm_g, tm=tm, tn=PROJ_W // 5, name="mixer_in_proj")
    ca8 = jnp.pad(ca_prev, ((0, 0), (6, 0), (0, 0)))
    cs8 = jnp.pad(cs_prev, ((0, 0), (5, 0), (0, 0)))
    st_t = jnp.transpose(st_prev, (0, 3, 1, 2)).reshape(b, SSM_STATE, SSM_INNER)
    mix, ca, cs, st = mixer_even(proj.reshape(b, l, PROJ_W), *mixer_w, ca8, cs8, st_t)
    x = rms_matmul(mix.reshape(b * l, -1), w_out, res=x, tm=tm, name="mixer_out_proj")
    st = jnp.transpose(st.reshape(b, SSM_STATE, SSM_HEADS, SSM_HEAD_DIM), (0, 2, 3, 1))
    return x, ca[:, 6:8], cs[:, 5:8], st


def _mem_kv_body(mem_ref, g_ref, wk_ref, wv_ref, kn_ref, k_ref, v_ref, kb_ref, vb_ref):
    mm = _rms(mem_ref[0], g_ref[...]).astype(BF16)
    k = jnp.dot(mm, wk_ref[...], preferred_element_type=F32)
    v = jnp.dot(mm, wv_ref[...], preferred_element_type=F32)
    for h in range(MEM_HEADS):
        sl = slice(h * MEM_HEAD_DIM, (h + 1) * MEM_HEAD_DIM)
        kh = _rms(k[:, sl], kn_ref[...])
        k_ref[0, :, sl] = kh
        kb_ref[0, :, sl] = kh.astype(BF16)
    v_ref[0] = v
    vb_ref[0] = v.astype(BF16)


def mem_kv(mem, src_norm, w_k, w_v, k_norm):
    B, Mt, D = mem.shape
    const = lambda shape: pl.BlockSpec(shape, lambda b: (0,) * len(shape))
    blk = pl.BlockSpec((1, Mt, D), lambda b: (b, 0, 0))
    return pl.pallas_call(
        _mem_kv_body,
        name="mem_kv",
        grid=(B,),
        in_specs=[blk, const((1, D)), const((D, D)), const((D, D)), const((1, MEM_HEAD_DIM))],
        out_specs=[blk, blk, blk, blk],
        out_shape=[jax.ShapeDtypeStruct((B, Mt, D), F32), jax.ShapeDtypeStruct((B, Mt, D), F32),
                   jax.ShapeDtypeStruct((B, Mt, D), BF16), jax.ShapeDtypeStruct((B, Mt, D), BF16)],
        compiler_params=_params("parallel"),
    )(mem, src_norm.reshape(1, D), w_k, w_v, k_norm.reshape(1, MEM_HEAD_DIM))


def _mem_attn_body(x_ref, g_ref, wq_ref, qn_ref, k_ref, v_ref, wo_ref, o_ref, *, nb):
    x = x_ref[...]
    tm = x.shape[0]
    rows = tm // nb
    hn = _rms(x, g_ref[...]).astype(BF16)
    q = jnp.dot(hn, wq_ref[...], preferred_element_type=F32)
    scale = MEM_HEAD_DIM ** -0.5
    outs = []
    for h in range(MEM_HEADS):
        sl = slice(h * MEM_HEAD_DIM, (h + 1) * MEM_HEAD_DIM)
        qh = _rms(q[:, sl], qn_ref[...]).astype(BF16)
        parts = []
        for s in range(nb):
            qs = qh[s * rows:(s + 1) * rows]
            sc = lax.dot_general(qs, k_ref[s, :, sl], NT_DIMS, preferred_element_type=F32) * scale
            sc = sc - jnp.max(sc, axis=-1, keepdims=True)
            p = jnp.exp(sc)
            p = p / jnp.sum(p, axis=-1, keepdims=True)
            parts.append(jnp.dot(p.astype(BF16), v_ref[s, :, sl], preferred_element_type=F32))
        outs.append(parts[0] if nb == 1 else jnp.concatenate(parts, axis=0))
    o = jnp.concatenate(outs, axis=1).astype(BF16)
    o_ref[...] = x + jnp.dot(o, wo_ref[...], preferred_element_type=F32)


def mem_attend(x, norm_g, k_b, v_b, w_q, q_norm, w_o, rows_per_seq, tm=512):
    M, D = x.shape
    tm = min(tm, M)
    nb = max(1, tm // rows_per_seq)
    tiles_per_seq = max(1, rows_per_seq // tm)
    const = lambda shape: pl.BlockSpec(shape, lambda i: (0,) * len(shape))
    kv_spec = pl.BlockSpec((nb, MEM_TOKENS, D), lambda i: (i // tiles_per_seq, 0, 0))
    return pl.pallas_call(
        functools.partial(_mem_attn_body, nb=nb),
        name="mem_attn",
        grid=(M // tm,),
        in_specs=[pl.BlockSpec((tm, D), lambda i: (i, 0)), const((1, D)), const((D, D)),
                  const((1, MEM_HEAD_DIM)), kv_spec, kv_spec, const((D, D))],
        out_specs=pl.BlockSpec((tm, D), lambda i: (i, 0)),
        out_shape=jax.ShapeDtypeStruct((M, D), F32),
        compiler_params=_params("parallel"),
    )(x, norm_g.reshape(1, D), w_q, q_norm.reshape(1, MEM_HEAD_DIM), k_b, v_b, w_o)


def _swiglu_step(xn, w1, w3, w2):
    h1 = jnp.dot(xn, w1, preferred_element_type=F32)
    h3 = jnp.dot(xn, w3, preferred_element_type=F32)
    a = (h1 * jax.nn.sigmoid(h1) * h3).astype(BF16)
    return jnp.dot(a, w2, preferred_element_type=F32)


def _ffn_dense_body(x_ref, g_ref, w1_ref, w3_ref, w2_ref, o_ref, xn_ref, acc_ref):
    f = pl.program_id(1)

    @pl.when(f == 0)
    def _():
        xn_ref[...] = _rms(x_ref[...], g_ref[...]).astype(BF16)
        acc_ref[...] = jnp.zeros_like(acc_ref)

    acc_ref[...] += _swiglu_step(xn_ref[...], w1_ref[...], w3_ref[...], w2_ref[...])

    @pl.when(f == pl.num_programs(1) - 1)
    def _():
        o_ref[...] = x_ref[...] + acc_ref[...]


def ffn_dense(x, norm_g, w1, w3, w2, tm=512, tf=None):
    M, D = x.shape
    F = w1.shape[1]
    tm = min(tm, M)
    tf = tf or F
    return pl.pallas_call(
        _ffn_dense_body,
        name="ffn_dense",
        grid=(M // tm, F // tf),
        in_specs=[pl.BlockSpec((tm, D), lambda i, f: (i, 0)), pl.BlockSpec((1, D), lambda i, f: (0, 0)),
                  pl.BlockSpec((D, tf), lambda i, f: (0, f)), pl.BlockSpec((D, tf), lambda i, f: (0, f)),
                  pl.BlockSpec((tf, D), lambda i, f: (f, 0))],
        out_specs=pl.BlockSpec((tm, D), lambda i, f: (i, 0)),
        out_shape=jax.ShapeDtypeStruct((M, D), F32),
        scratch_shapes=[pltpu.VMEM((tm, D), BF16), pltpu.VMEM((tm, D), F32)],
        compiler_params=_params("parallel", "arbitrary"),
    )(x, norm_g.reshape(1, D), w1, w3, w2)


def _ffn_expert_body(te_ref, tv_ref, x_ref, w1_ref, w3_ref, w2_ref, o_ref, xn_ref, acc_ref):
    i = pl.program_id(0)
    f = pl.program_id(1)
    last = f == pl.num_programs(1) - 1

    @pl.when(tv_ref[i] > 0)
    def _():
        @pl.when(f == 0)
        def _():
            xn_ref[...] = x_ref[...].astype(BF16)
            acc_ref[...] = jnp.zeros_like(acc_ref)

        acc_ref[...] += _swiglu_step(xn_ref[...], w1_ref[0], w3_ref[0], w2_ref[0])

        @pl.when(last)
        def _():
            o_ref[...] = acc_ref[...]

    @pl.when(jnp.logical_and(tv_ref[i] == 0, last))
    def _():
        o_ref[...] = jnp.zeros_like(o_ref)


def ffn_experts(xs, tile_expert, tile_valid, w1, w3, w2, tm, tf=1792):
    N, D = xs.shape
    F = w1.shape[2]
    grid_spec = pltpu.PrefetchScalarGridSpec(
        num_scalar_prefetch=2,
        grid=(N // tm, F // tf),
        in_specs=[pl.BlockSpec((tm, D), lambda i, f, te, tv: (i, 0)),
                  pl.BlockSpec((1, D, tf), lambda i, f, te, tv: (te[i], 0, f)),
                  pl.BlockSpec((1, D, tf), lambda i, f, te, tv: (te[i], 0, f)),
                  pl.BlockSpec((1, tf, D), lambda i, f, te, tv: (te[i], f, 0))],
        out_specs=pl.BlockSpec((tm, D), lambda i, f, te, tv: (i, 0)),
        scratch_shapes=[pltpu.VMEM((tm, D), BF16), pltpu.VMEM((tm, D), F32)],
    )
    return pl.pallas_call(
        _ffn_expert_body,
        name="ffn_experts",
        grid_spec=grid_spec,
        out_shape=jax.ShapeDtypeStruct((N, D), F32),
        compiler_params=_params("arbitrary", "arbitrary"),
    )(tile_expert, tile_valid, xs, w1, w3, w2)


def _router_body(x_ref, g_ref, wr_ref, route_ref):
    hn = _rms(x_ref[...], g_ref[...])
    xh = hn.astype(BF16)
    xm = (hn - xh.astype(F32)).astype(BF16)
    a = jnp.dot(xh, wr_ref[...], preferred_element_type=F32)
    b = jnp.dot(xm, wr_ref[...], preferred_element_type=F32)
    logits = (a[:, :128] + a[:, 128:]) + (b[:, :128] + b[:, 128:])
    lane = lax.broadcasted_iota(jnp.int32, logits.shape, 1)
    lane_f = lane.astype(F32)
    neg = -jnp.inf
    lg = jnp.where(lane < N_EXPERTS, logits, neg)
    m1 = jnp.max(lg, axis=-1, keepdims=True)
    i1 = jnp.min(jnp.where(lg == m1, lane_f, 128.0), axis=-1, keepdims=True)
    lg2 = jnp.where(lane_f == i1, neg, lg)
    m2 = jnp.max(lg2, axis=-1, keepdims=True)
    i2 = jnp.min(jnp.where(lg2 == m2, lane_f, 128.0), axis=-1, keepdims=True)
    e = jnp.exp(m2 - m1)
    den = 1.0 + e
    g1 = 1.0 / den
    g2 = e / den
    route = jnp.where(lane == 0, g1, jnp.where(lane == 1, g2, jnp.where(lane == 2, i1, jnp.where(lane == 3, i2, 0.0))))
    route_ref[...] = route


def router(x, norm_g, w_router, tm=512):
    M, D = x.shape
    tm = min(tm, M)
    wr = jnp.pad(w_router.astype(F32), ((0, 0), (0, 128 - N_EXPERTS)))
    wh = wr.astype(BF16)
    wm = (wr - wh.astype(F32)).astype(BF16)
    w2 = jnp.concatenate([wh, wm], axis=1)
    return pl.pallas_call(
        _router_body,
        name="router",
        grid=(M // tm,),
        in_specs=[pl.BlockSpec((tm, D), lambda i: (i, 0)), pl.BlockSpec((1, D), lambda i: (0, 0)),
                  pl.BlockSpec((D, 256), lambda i: (0, 0))],
        out_specs=pl.BlockSpec((tm, 128), lambda i: (i, 0)),
        out_shape=jax.ShapeDtypeStruct((M, 128), F32),
        compiler_params=_params("parallel"),
    )(x, norm_g.reshape(1, D), w2)


def _row_put(src_ref, r, dst_hbm, idx, sem):
    return pltpu.make_async_copy(src_ref.at[pl.ds(r, 1), :], dst_hbm.at[pl.ds(idx, 1), :], sem)


def _row_get(src_hbm, idx, dst_ref, r, sem):
    return pltpu.make_async_copy(src_hbm.at[pl.ds(idx, 1), :], dst_ref.at[pl.ds(r, 1), :], sem)


def _dispatch_body(i1_ref, i2_ref, x_ref, g_ref, xs_in_hbm, xs_hbm, hn_ref, sem):
    del xs_in_hbm
    tm = hn_ref.shape[0]
    hn_ref[...] = _rms(x_ref[...], g_ref[...])

    def start(r, _):
        _row_put(hn_ref, r, xs_hbm, i1_ref[0, 0, r], sem).start()
        _row_put(hn_ref, r, xs_hbm, i2_ref[0, 0, r], sem).start()
        return 0

    lax.fori_loop(0, tm, start, 0, unroll=8)

    def wait(r, _):
        _row_put(hn_ref, r, xs_hbm, 0, sem).wait()
        _row_put(hn_ref, r, xs_hbm, 0, sem).wait()
        return 0

    lax.fori_loop(0, tm, wait, 0, unroll=8)


def dispatch_rows(x, norm_g, pos1, pos2, xs, tm=512):
    M, D = x.shape
    tm = min(tm, M)
    idx_spec = pl.BlockSpec((1, 1, tm), lambda i: (i, 0, 0), memory_space=pltpu.SMEM)
    return pl.pallas_call(
        _dispatch_body,
        name="dispatch_rows",
        grid=(M // tm,),
        in_specs=[idx_spec, idx_spec, pl.BlockSpec((tm, D), lambda i: (i, 0)), pl.BlockSpec((1, D), lambda i: (0, 0)),
                  pl.BlockSpec(memory_space=pl.ANY)],
        out_specs=pl.BlockSpec(memory_space=pl.ANY),
        out_shape=jax.ShapeDtypeStruct(xs.shape, xs.dtype),
        input_output_aliases={4: 0},
        scratch_shapes=[pltpu.VMEM((tm, D), F32), pltpu.SemaphoreType.DMA(())],
        compiler_params=_params("arbitrary"),
    )(pos1.reshape(M // tm, 1, tm), pos2.reshape(M // tm, 1, tm), x, norm_g.reshape(1, D), xs)


def _combine_body(i1_ref, i2_ref, x_ref, route_ref, ys_hbm, o_ref, b1, b2, sem1, sem2):
    tm = o_ref.shape[0]

    def start(r, _):
        _row_get(ys_hbm, i1_ref[0, 0, r], b1, r, sem1).start()
        _row_get(ys_hbm, i2_ref[0, 0, r], b2, r, sem2).start()
        return 0

    lax.fori_loop(0, tm, start, 0, unroll=8)

    def wait(r, _):
        _row_get(ys_hbm, 0, b1, r, sem1).wait()
        _row_get(ys_hbm, 0, b2, r, sem2).wait()
        return 0

    lax.fori_loop(0, tm, wait, 0, unroll=8)
    o_ref[...] = x_ref[...] + route_ref[:, 0:1] * b1[...] + route_ref[:, 1:2] * b2[...]


def combine_rows(x, route, ys, pos1, pos2, tm=256):
    M, D = x.shape
    tm = min(tm, M)
    idx_spec = pl.BlockSpec((1, 1, tm), lambda i: (i, 0, 0), memory_space=pltpu.SMEM)
    return pl.pallas_call(
        _combine_body,
        name="combine_rows",
        grid=(M // tm,),
        in_specs=[idx_spec, idx_spec, pl.BlockSpec((tm, D), lambda i: (i, 0)),
                  pl.BlockSpec((tm, 128), lambda i: (i, 0)), pl.BlockSpec(memory_space=pl.ANY)],
        out_specs=pl.BlockSpec((tm, D), lambda i: (i, 0)),
        out_shape=jax.ShapeDtypeStruct((M, D), F32),
        scratch_shapes=[pltpu.VMEM((tm, D), F32), pltpu.VMEM((tm, D), F32),
                        pltpu.SemaphoreType.DMA(()), pltpu.SemaphoreType.DMA(())],
        compiler_params=_params("arbitrary"),
    )(pos1.reshape(M // tm, 1, tm), pos2.reshape(M // tm, 1, tm), x, route, ys)


def moe_ffn(xs_list, norm_g, w_router, w1, w3, w2, tm_e):
    D = xs_list[0].shape[1]
    routes = [router(x, norm_g, w_router) for x in xs_list]
    top_i = jnp.concatenate([route[:, 2:4] for route in routes], axis=0).astype(jnp.int32)
    T = top_i.shape[0]
    e_flat = top_i.reshape(-1)
    onehot = (e_flat[:, None] == jnp.arange(N_EXPERTS, dtype=jnp.int32)[None, :]).astype(jnp.int32)
    csum = jnp.cumsum(onehot, axis=0)
    counts = csum[-1]
    rank = jnp.take_along_axis(csum, e_flat[:, None], axis=1)[:, 0] - 1
    ptiles = (counts + tm_e - 1) // tm_e
    tile_end = jnp.cumsum(ptiles)
    pos = (tile_end - ptiles)[e_flat] * tm_e + rank
    pos1, pos2 = pos[0::2], pos[1::2]
    n_tiles = (2 * T) // tm_e + N_EXPERTS
    tile_ids = jnp.arange(n_tiles, dtype=jnp.int32)
    tile_valid = (tile_ids < tile_end[-1]).astype(jnp.int32)
    last_e = jnp.max(jnp.where(counts > 0, jnp.arange(N_EXPERTS, dtype=jnp.int32), 0))
    tile_expert = jnp.sum((tile_end[None, :] <= tile_ids[:, None]).astype(jnp.int32), axis=1)
    tile_expert = jnp.minimum(tile_expert, last_e)
    xs = jnp.zeros((n_tiles * tm_e, D), F32)
    t0 = 0
    for x in xs_list:
        t1 = t0 + x.shape[0]
        xs = dispatch_rows(x, norm_g, pos1[t0:t1], pos2[t0:t1], xs)
        t0 = t1
    ys = ffn_experts(xs, tile_expert, tile_valid, w1, w3, w2, tm_e)
    outs = []
    t0 = 0
    for x, route in zip(xs_list, routes):
        t1 = t0 + x.shape[0]
        outs.append(combine_rows(x, route, ys, pos1[t0:t1], pos2[t0:t1]))
        t0 = t1
    return outs


def _rope_lanes(v, cos, sin):
    return v * cos + pltpu.roll(v, 112, 1) * sin


def _mla_rows_body(x_ref, g_ref, wd_ref, qln_ref, kvln_ref, krn_ref, wuq_ref, qn_ref, cos_ref, sin_ref,
                   q_ref, ckv_ref, kr_ref):
    hn = _rms(x_ref[...], g_ref[...]).astype(BF16)
    d = jnp.dot(hn, wd_ref[...], preferred_element_type=F32)
    cq = _rms(d[:, 0:MLA_Q_LORA], qln_ref[...]).astype(BF16)
    ckv_ref[...] = _rms(d[:, MLA_Q_LORA:MLA_Q_LORA + MLA_KV_LORA], kvln_ref[...])
    cos = cos_ref[...]
    sin = sin_ref[...]
    lane = lax.broadcasted_iota(jnp.int32, cos.shape, 1)
    kr = d[:, MLA_Q_LORA + MLA_KV_LORA:]
    kr = kr * lax.rsqrt(jnp.sum(kr * kr, axis=-1, keepdims=True) * (0.5 / MLA_ROPE) + EPS) * krn_ref[...]
    kr_ref[...] = _rope_lanes(kr, cos, sin)
    q = jnp.dot(cq, wuq_ref[...], preferred_element_type=F32)
    nope = lane < MLA_NOPE
    for h in range(MLA_HEADS):
        qh = q[:, h * 128:(h + 1) * 128]
        sq = qh * qh
        ss_n = jnp.sum(jnp.where(nope, sq, 0.0), axis=-1, keepdims=True) * (1.0 / MLA_NOPE)
        ss_r = jnp.sum(jnp.where(nope, 0.0, sq), axis=-1, keepdims=True) * (0.5 / MLA_ROPE)
        r = jnp.where(nope, lax.rsqrt(ss_n + EPS), lax.rsqrt(ss_r + EPS))
        qh = _rope_lanes(qh * r * qn_ref[...], cos, sin)
        q_ref[:, h * 128:(h + 1) * 128] = (qh * (MLA_SCALE * math.log2(math.e))).astype(BF16)


def mla_rows(x, norm_g, wd_pad, q_lora_norm, kv_lora_norm, krn_pad, wuq_pad, qn_pad, cos_t, sin_t, tm=512):
    M, D = x.shape
    tm = min(tm, M)
    const = lambda shape: pl.BlockSpec(shape, lambda i: (0,) * len(shape))
    rows = lambda n: pl.BlockSpec((tm, n), lambda i: (i, 0))
    table_tiles = cos_t.shape[0] // tm
    table = pl.BlockSpec((tm, 128), lambda i: (i % table_tiles, 0))
    return pl.pallas_call(
        _mla_rows_body,
        name="mla_rows",
        grid=(M // tm,),
        in_specs=[rows(D), const((1, D)), const(wd_pad.shape), const((1, MLA_Q_LORA)), const((1, MLA_KV_LORA)),
                  const((1, 128)), const(wuq_pad.shape), const((1, 128)), table, table],
        out_specs=[rows(MLA_HEADS * 128), rows(MLA_KV_LORA), rows(128)],
        out_shape=[jax.ShapeDtypeStruct((M, MLA_HEADS * 128), BF16), jax.ShapeDtypeStruct((M, MLA_KV_LORA), F32),
                   jax.ShapeDtypeStruct((M, 128), F32)],
        compiler_params=_params("parallel"),
    )(x, norm_g.reshape(1, D), wd_pad, q_lora_norm.reshape(1, -1), kv_lora_norm.reshape(1, -1), krn_pad,
      wuq_pad, qn_pad, cos_t, sin_t)


def _mla_kv_body(ckv_ref, kr_ref, wuk_ref, wuv_ref, kn_ref, k_ref, v_ref):
    c = ckv_ref[...].astype(BF16)
    kn = jnp.dot(c, wuk_ref[...], preferred_element_type=F32)
    lane = lax.broadcasted_iota(jnp.int32, (1, MLA_HEADS * 128), 1)
    ones_col = jnp.where(jnp.bitwise_and(lane, 127) == MLA_V, 1.0, 0.0)
    v_ref[...] = (jnp.dot(c, wuv_ref[...], preferred_element_type=F32) + ones_col).astype(BF16)
    kr = kr_ref[...]
    for h in range(MLA_HEADS):
        kh = kn[:, h * 128:(h + 1) * 128]
        r = lax.rsqrt(jnp.sum(kh * kh, axis=-1, keepdims=True) * (1.0 / MLA_NOPE) + EPS)
        k_ref[:, h * 128:(h + 1) * 128] = (kh * r * kn_ref[...] + kr).astype(BF16)


def mla_kv(ckv, kr_pad, wuk_pad, wuv, kn_pad, tm=512):
    M = ckv.shape[0]
    tm = min(tm, M)
    const = lambda shape: pl.BlockSpec(shape, lambda i: (0,) * len(shape))
    rows = lambda n: pl.BlockSpec((tm, n), lambda i: (i, 0))
    return pl.pallas_call(
        _mla_kv_body,
        name="mla_kv",
        grid=(M // tm,),
        in_specs=[rows(MLA_KV_LORA), rows(128), const(wuk_pad.shape), const(wuv.shape), const((1, 128))],
        out_specs=[rows(MLA_HEADS * 128), rows(MLA_HEADS * 128)],
        out_shape=[jax.ShapeDtypeStruct((M, MLA_HEADS * 128), BF16),
                   jax.ShapeDtypeStruct((M, MLA_HEADS * 128), BF16)],
        compiler_params=_params("parallel"),
    )(ckv, kr_pad, wuk_pad, wuv, kn_pad)


def _flash_body(qi_ref, ki_ref, q_ref, k_ref, v_ref, *rest, causal):
    if causal:
        bias_ref, o_ref, m_ref, acc_ref = rest
    else:
        o_ref, m_ref, acc_ref = rest
    t = pl.program_id(2)
    qi = qi_ref[t]
    ki = ki_ref[t]
    tq = q_ref.shape[1]

    @pl.when(ki == 0)
    def _():
        m_ref[...] = jnp.full_like(m_ref, -jnp.inf)
        acc_ref[...] = jnp.zeros_like(acc_ref)

    def step(masked):
        tk = k_ref.shape[1]
        sub = min(FLASH_KV_SUB, tk) if masked else tk
        for h in range(FLASH_HEADS):
            hl = slice(h * 128, (h + 1) * 128)
            m, acc = m_ref[h], acc_ref[h]
            for j in range(tk // sub):
                r0 = j * sub if masked else 0
                cols = slice(j * sub, (j + 1) * sub)
                s = lax.dot_general(q_ref[0, r0:, hl], k_ref[0, cols, hl], NT_DIMS, preferred_element_type=F32)
                if masked:
                    s = s + bias_ref[r0:, cols]
                m_new = jnp.maximum(m[r0:], jnp.broadcast_to(jnp.max(s, axis=-1, keepdims=True), (tq - r0, 128)))
                alpha = jnp.exp2(m[r0:] - m_new)
                m_wide = jnp.concatenate([m_new] * (sub // 128), axis=1) if sub % 128 == 0 else m_new[:, 0:1]
                p = jnp.exp2(s - m_wide)
                acc_new = alpha * acc[r0:] + jnp.dot(p.astype(BF16), v_ref[0, cols, hl], preferred_element_type=F32)
                if r0:
                    m_new = jnp.concatenate([m[:r0], m_new], axis=0)
                    acc_new = jnp.concatenate([acc[:r0], acc_new], axis=0)
                m, acc = m_new, acc_new
            m_ref[h], acc_ref[h] = m, acc

    def finish():
        lane = lax.broadcasted_iota(jnp.int32, (tq, 128), 1)
        for h in range(0, FLASH_HEADS, 2):
            o0 = acc_ref[h] / acc_ref[h, :, MLA_V:MLA_V + 1]
            o1 = acc_ref[h + 1] / acc_ref[h + 1, :, MLA_V:MLA_V + 1]
            both = jnp.where(lane < MLA_V, o0, pltpu.roll(o1, MLA_V, 1))
            o_ref[0, :, h * MLA_V:(h + 2) * MLA_V] = both.astype(o_ref.dtype)

    if causal:
        @pl.when(ki < qi)
        def _():
            step(False)

        @pl.when(ki == qi)
        def _():
            step(True)
            finish()
    else:
        step(False)

        @pl.when(t == pl.num_programs(2) - 1)
        def _():
            finish()


def mla_attention(q, k, v, causal, tq, tk):
    B, Lq, _ = q.shape
    Lk = k.shape[1]
    nq, nk = Lq // tq, Lk // tk
    pairs = [(i, j) for i in range(nq) for j in range(nk) if (j <= i or not causal)]
    qi_tab = jnp.asarray([i for i, _ in pairs], jnp.int32)
    ki_tab = jnp.asarray([j for _, j in pairs], jnp.int32)
    hw = FLASH_HEADS * 128
    in_specs = [pl.BlockSpec((1, tq, hw), lambda b, p, t, qt, kt: (b, qt[t], p)),
                pl.BlockSpec((1, tk, hw), lambda b, p, t, qt, kt: (b, kt[t], p)),
                pl.BlockSpec((1, tk, hw), lambda b, p, t, qt, kt: (b, kt[t], p))]
    args = [q, k, v]
    if causal:
        chunk = jnp.arange(tq) // CHUNK
        bias = jnp.where(chunk[None, :] <= chunk[:, None], 0.0, -jnp.inf).astype(F32)
        in_specs.append(pl.BlockSpec((tq, tk), lambda b, p, t, qt, kt: (0, 0)))
        args.append(bias)
    grid_spec = pltpu.PrefetchScalarGridSpec(
        num_scalar_prefetch=2,
        grid=(B, MLA_HEADS // FLASH_HEADS, len(pairs)),
        in_specs=in_specs,
        out_specs=pl.BlockSpec((1, tq, FLASH_HEADS * MLA_V), lambda b, p, t, qt, kt: (b, qt[t], p)),
        scratch_shapes=[pltpu.VMEM((FLASH_HEADS, tq, 128), F32), pltpu.VMEM((FLASH_HEADS, tq, 128), F32)],
    )
    return pl.pallas_call(
        functools.partial(_flash_body, causal=causal),
        name="mla_flash",
        grid_spec=grid_spec,
        out_shape=jax.ShapeDtypeStruct((B, Lq, MLA_HEADS * MLA_V), BF16),
        compiler_params=_params("parallel", "parallel", "arbitrary"),
    )(qi_tab, ki_tab, *args)


def _rope_tables(pos):
    half = MLA_ROPE // 2
    inv = ROPE_THETA ** (-jnp.arange(half, dtype=F32) / half)
    ang = pos.astype(F32)[:, None] * inv[None, :]
    cos, sin = jnp.cos(ang), jnp.sin(ang)
    n = pos.shape[0]
    ones = jnp.ones((n, MLA_NOPE), F32)
    zeros = jnp.zeros((n, 128 - MLA_NOPE - MLA_ROPE), F32)
    cos_t = jnp.concatenate([ones, cos, cos, zeros], axis=1)
    sin_t = jnp.concatenate([jnp.zeros((n, MLA_NOPE), F32), -sin, sin, zeros], axis=1)
    return cos_t, sin_t


def _pad_heads(w, d_in, d_head, d_pad):
    w = w.reshape(d_in, MLA_HEADS, d_head)
    return jnp.pad(w, ((0, 0), (0, 0), (0, d_pad - d_head))).reshape(d_in, MLA_HEADS * d_pad)


def mla_weights(norm_g, w_down, q_lora_norm, kv_lora_norm, w_uq, w_ukv, q_nope_norm, q_rope_norm, k_nope_norm,
                k_rope_norm, w_o):
    D = w_down.shape[0]
    split = MLA_Q_LORA + MLA_KV_LORA
    wd_pad = jnp.concatenate([w_down[:, :split], jnp.zeros((D, MLA_NOPE), F32), w_down[:, split:],
                              w_down[:, split:]], axis=1).astype(BF16)
    krn_pad = jnp.concatenate([jnp.zeros((MLA_NOPE,), F32), k_rope_norm, k_rope_norm]).reshape(1, 128)
    qn_pad = jnp.concatenate([q_nope_norm, q_rope_norm, q_rope_norm]).reshape(1, 128)
    kn_pad = jnp.concatenate([k_nope_norm, jnp.zeros((128 - MLA_NOPE,), F32)]).reshape(1, 128)
    wuq3 = w_uq.reshape(MLA_Q_LORA, MLA_HEADS, MLA_NOPE + MLA_ROPE)
    wuq_pad = jnp.concatenate([wuq3, wuq3[:, :, MLA_NOPE:]], axis=2).reshape(MLA_Q_LORA, MLA_HEADS * 128).astype(BF16)
    wukv = w_ukv.reshape(MLA_KV_LORA, MLA_HEADS, MLA_NOPE + MLA_V)
    wuk_pad = _pad_heads(wukv[:, :, :MLA_NOPE].reshape(MLA_KV_LORA, -1), MLA_KV_LORA, MLA_NOPE, 128).astype(BF16)
    wuv = _pad_heads(wukv[:, :, MLA_NOPE:].reshape(MLA_KV_LORA, -1), MLA_KV_LORA, MLA_V, 128).astype(BF16)
    rows_w = (norm_g, wd_pad, q_lora_norm, kv_lora_norm, krn_pad, wuq_pad, qn_pad)
    return rows_w, (wuk_pad, wuv, kn_pad), w_o.astype(BF16)


def mla_block(x, b, l, mw, ckv_past, kr_past, tile):
    rows_w, kv_w, w_o = mw
    n_past = 0 if ckv_past is None else ckv_past.shape[1]
    cos_t, sin_t = _rope_tables(n_past + jnp.arange(l))
    reps = max(1, min(b * l, 512) // l)
    q, ckv, kr = mla_rows(x, *rows_w, jnp.tile(cos_t, (reps, 1)), jnp.tile(sin_t, (reps, 1)))
    kr_out = kr[:, MLA_NOPE:MLA_NOPE + MLA_ROPE]
    if ckv_past is None:
        k_all, v_all = mla_kv(ckv, kr, *kv_w)
        o = mla_attention(q.reshape(b, l, -1), k_all.reshape(b, l, -1), v_all.reshape(b, l, -1), True, tile, tile)
    else:
        lk = n_past + l
        ckv_all = jnp.concatenate([ckv_past, ckv.reshape(b, l, -1)], axis=1).reshape(b * lk, -1)
        kr_pad = jnp.pad(kr_past, ((0, 0), (0, 0), (MLA_NOPE, 128 - MLA_NOPE - MLA_ROPE)))
        kr_all = jnp.concatenate([kr_pad, kr.reshape(b, l, 128)], axis=1).reshape(b * lk, 128)
        k_all, v_all = mla_kv(ckv_all, kr_all, *kv_w, tm=lk // 5)
        o = mla_attention(q.reshape(b, l, -1), k_all.reshape(b, lk, -1), v_all.reshape(b, lk, -1), False, l, lk)
    x = rms_matmul(o.reshape(b * l, -1), w_o, res=x, tm=1024, name="mla_out_proj")
    return x, ckv, kr_out


def kernel(x_prompt, x_sample, mem_prompt, state_conv_a, state_conv_ssm, state_ssm, cache_mla_ckv, cache_mla_krope, cache_mem_k, cache_mem_v, norm_mix, norm_mem, norm_ffn, w_in_e, conv_a_w, conv_s_w, conv_s_b, dt_bias, a_log, d_skip, ssm_norm, w_out_e, w_down_o, q_lora_norm, kv_lora_norm, w_uq, w_ukv, q_nope_norm, q_rope_norm, k_nope_norm, k_rope_norm, w_o_mla, mem_src_norm, w_mem_q, w_mem_k, w_mem_v, mem_q_norm, mem_k_norm, w_mem_o, w_ffn1, w_ffn3, w_ffn2, w_router, w_exp1, w_exp3, w_exp2):
    bp, lp, D = x_prompt.shape
    bs, ls, _ = x_sample.shape
    past = cache_mla_ckv.shape[2]
    bf = lambda w: w.astype(BF16)
    xp = x_prompt.reshape(bp * lp, D)
    xs = x_sample.reshape(bs * ls, D)

    ew = even_weights(w_in_e[0], w_out_e[0], norm_mix[0], conv_a_w[0], conv_s_w[0], conv_s_b[0], dt_bias[0],
                      a_log[0], d_skip[0], ssm_norm[0])
    xp, p_ca, p_cs, p_st = even_mixer_block(
        xp, bp, lp, ew, jnp.zeros((bp, 2, A_WIDTH), F32), jnp.zeros((bp, 3, SSM_XBC), F32),
        jnp.zeros((bp, SSM_HEADS, SSM_HEAD_DIM, SSM_STATE), F32), 1024)
    xs, s_ca, s_cs, s_st = even_mixer_block(xs, bs, ls, ew, state_conv_a[0], state_conv_ssm[0], state_ssm[0], 512)

    p_mk, p_mv = [], []

    def memory_block(i, xp, xs):
        mk, mv, mk_b, mv_b = mem_kv(mem_prompt, mem_src_norm[i], bf(w_mem_k[i]), bf(w_mem_v[i]), mem_k_norm[i])
        p_mk.append(mk.reshape(bp, MEM_TOKENS, MEM_HEADS, MEM_HEAD_DIM))
        p_mv.append(mv.reshape(bp, MEM_TOKENS, MEM_HEADS, MEM_HEAD_DIM))
        wq, wo = bf(w_mem_q[i]), bf(w_mem_o[i])
        xp = mem_attend(xp, norm_mem[i], mk_b, mv_b, wq, mem_q_norm[i], wo, lp)
        ck = bf(cache_mem_k[i]).reshape(bs, MEM_TOKENS, D)
        cv = bf(cache_mem_v[i]).reshape(bs, MEM_TOKENS, D)
        xs = mem_attend(xs, norm_mem[i], ck, cv, wq, mem_q_norm[i], wo, ls)
        return xp, xs

    xp, xs = memory_block(0, xp, xs)
    w1, w3, w2 = bf(w_ffn1[0]), bf(w_ffn3[0]), bf(w_ffn2[0])
    xp = ffn_dense(xp, norm_ffn[0], w1, w3, w2)
    xs = ffn_dense(xs, norm_ffn[0], w1, w3, w2)

    mw = mla_weights(norm_mix[1], w_down_o[0], q_lora_norm[0], kv_lora_norm[0], w_uq[0], w_ukv[0], q_nope_norm[0],
                     q_rope_norm[0], k_nope_norm[0], k_rope_norm[0], w_o_mla[0])
    xp, p_ckv, p_kr = mla_block(xp, bp, lp, mw, None, None, min(lp, 1024))
    xs, s_ckv, s_kr = mla_block(xs, bs, ls, mw, cache_mla_ckv[0], cache_mla_krope[0], None)

    xp, xs = memory_block(1, xp, xs)
    we1, we3, we2 = bf(w_exp1[0]), bf(w_exp3[0]), bf(w_exp2[0])
    xp, xs = moe_ffn([xp, xs], norm_ffn[1], w_router[0], we1, we3, we2, 512)

    return (xp.reshape(bp, lp, D), xs.reshape(bs, ls, D),
            p_ca[None], p_cs[None], p_st[None],
            p_ckv.reshape(1, bp, lp, -1), p_kr.reshape(1, bp, lp, -1), jnp.stack(p_mk), jnp.stack(p_mv),
            s_ca[None], s_cs[None], s_st[None],
            s_ckv.reshape(1, bs, ls, -1), s_kr.reshape(1, bs, ls, -1))
```
